```python
import jax, jax.numpy as jnp
from jax import lax
import numpy as np

D_MODEL = 4096
BATCH = 1
SEQ = 16384
DEPTH = 4

GRID_W = 64
CTX_LEN = 256
D_MIX = D_MODEL
W_CONV = D_MIX // 4
W_REC = D_MIX // 4
W_ATT = D_MIX - W_CONV - W_REC
CONV_K = 31
REC_HEADS = 8
REC_BLOCK = W_REC // REC_HEADS
REC_CONV_K = 4
REC_C = 8.0
ATT_HEAD_DIM = 128
ATT_HEADS = W_ATT // ATT_HEAD_DIM
WIN_H = 8
WIN_W = 16
SPLIT_SIZES = (W_CONV, W_CONV, W_CONV, W_REC, W_REC, W_ATT, W_ATT, W_ATT, W_ATT)
N_IN = sum(SPLIT_SIZES)
EPS = 1e-6

kernel_name = "hybrid_conv_rglru_natten_prefix_dit"


def rmsnorm(x, g):
    xf = x.astype(jnp.float32)
    y = xf * lax.rsqrt(jnp.mean(xf * xf, axis=-1, keepdims=True) + EPS)
    return (y * g.astype(jnp.float32)).astype(x.dtype)


def depthwise_conv(x, w, b, pad):
    y = lax.conv_general_dilated(x, w[:, None, :].astype(x.dtype), (1,), [pad],
                                 dimension_numbers=('NWC', 'WIO', 'NWC'),
                                 feature_group_count=x.shape[-1])
    return y + b.astype(x.dtype)


def conformer_conv(val, glu_gate, w_dw, b_dw, ln_g, ln_b, w_pw, b_pw):
    u = val * jax.nn.sigmoid(glu_gate)
    u = depthwise_conv(u, w_dw, b_dw, (CONV_K // 2, CONV_K // 2))
    uf = u.astype(jnp.float32)
    mu = jnp.mean(uf, axis=-1, keepdims=True)
    var = jnp.mean(jnp.square(uf - mu), axis=-1, keepdims=True)
    uf = (uf - mu) * lax.rsqrt(var + EPS) * ln_g.astype(jnp.float32) + ln_b.astype(jnp.float32)
    u = jax.nn.silu(uf).astype(val.dtype)
    return u @ w_pw + b_pw


def linear_scan(a, b, h0):
    b = b.at[:, 0].add(a[:, 0] * h0)
    def combine(l, r):
        return (l[0] * r[0], r[0] * l[1] + r[1])
    _, h = lax.associative_scan(combine, (a, b), axis=1)
    return h


def rglru_direction(x, conv_w, conv_b, w_r, b_r, w_i, b_i, lam, h0, reverse):
    pad = (0, REC_CONV_K - 1) if reverse else (REC_CONV_K - 1, 0)
    xc = depthwise_conv(x, conv_w, conv_b, pad)
    bsz, s_len, _ = xc.shape
    xh = xc.reshape(bsz, s_len, REC_HEADS, REC_BLOCK)
    r = jax.nn.sigmoid((jnp.einsum('bshk,hkj->bshj', xh, w_r).reshape(bsz, s_len, W_REC) + b_r).astype(jnp.float32))
    i = jax.nn.sigmoid((jnp.einsum('bshk,hkj->bshj', xh, w_i).reshape(bsz, s_len, W_REC) + b_i).astype(jnp.float32))
    log_a = -REC_C * r * jax.nn.softplus(-lam.astype(jnp.float32))
    a = jnp.exp(log_a)
    gated = jnp.sqrt(-jnp.expm1(2.0 * log_a)) * (i * xc.astype(jnp.float32))
    if reverse:
        a, gated = a[:, ::-1], gated[:, ::-1]
    h = linear_scan(a, gated, h0)
    h_last = h[:, -1]
    if reverse:
        h = h[:, ::-1]
    return h.astype(x.dtype), h_last


def neighbourhood_attention(q, k, v, k_ctx, v_ctx, rpb):
    bsz, s_len, n_h, d_h = q.shape
    rows = s_len // GRID_W
    kh = min(WIN_H, rows)
    scale = d_h ** -0.5
    qg = q.reshape(bsz, rows, GRID_W, n_h, d_h)
    kg = k.reshape(bsz, rows, GRID_W, n_h, d_h)
    vg = v.reshape(bsz, rows, GRID_W, n_h, d_h)
    cols = np.arange(GRID_W)
    col_start = np.clip(cols - WIN_W // 2, 0, GRID_W - WIN_W)
    col_idx = col_start[:, None] + np.arange(WIN_W)[None, :]
    col_bias_idx = col_idx - cols[:, None] + (WIN_W - 1)
    n_loc = kh * WIN_W

    def one_row(r):
        rs = jnp.clip(r - WIN_H // 2, 0, rows - kh)
        q_r = lax.dynamic_index_in_dim(qg, r, axis=1, keepdims=False)
        k_rows = lax.dynamic_slice_in_dim(kg, rs, kh, axis=1)
        v_rows = lax.dynamic_slice_in_dim(vg, rs, kh, axis=1)
        k_win = k_rows[:, :, col_idx]
        v_win = v_rows[:, :, col_idx]
        s_loc = jnp.einsum('bqhd,bkqjhd->bhqkj', q_r, k_win).astype(jnp.float32) * scale
        row_bias_idx = rs + jnp.arange(kh) - r + (WIN_H - 1)
        bias = rpb[:, row_bias_idx[None, :, None], col_bias_idx[:, None, :]]
        s_loc = s_loc + bias.astype(jnp.float32)[None]
        s_ctx = jnp.einsum('bqhd,blhd->bhql', q_r, k_ctx).astype(jnp.float32) * scale
        s = jnp.concatenate([s_loc.reshape(bsz, n_h, GRID_W, n_loc), s_ctx], axis=-1)
        p = jax.nn.softmax(s, axis=-1).astype(q.dtype)
        p_loc = p[..., :n_loc].reshape(bsz, n_h, GRID_W, kh, WIN_W)
        p_ctx = p[..., n_loc:]
        return (jnp.einsum('bhqkj,bkqjhd->bqhd', p_loc, v_win)
                + jnp.einsum('bhql,blhd->bqhd', p_ctx, v_ctx))

    out = lax.map(one_row, jnp.arange(rows))
    return jnp.moveaxis(out, 0, 1).reshape(bsz, s_len, n_h * d_h)


def context_attention(q, k, v):
    bsz, l_len, n_h, d_h = q.shape
    s = jnp.einsum('bqhd,bkhd->bhqk', q, k).astype(jnp.float32) * (d_h ** -0.5)
    p = jax.nn.softmax(s, axis=-1).astype(q.dtype)
    return jnp.einsum('bhqk,bkhd->bqhd', p, v).reshape(bsz, l_len, n_h * d_h)


def split_proj(z):
    idx = [int(v) for v in np.cumsum(SPLIT_SIZES)[:-1]]
    return jnp.split(z, idx, axis=-1)


def hybrid_layer(xl, xc, c, c_ctx, w_ada, b_ada, norm_g, w_in, conv_w, conv_b, ln_g, ln_b,
                 w_pw, b_pw, rconv_w, rconv_b, w_r, b_r, w_i, b_i, lam, rpb, w_out, need_ctx):
    bsz = xl.shape[0]
    mod_l = jax.nn.silu(c) @ w_ada + b_ada
    mod_c = jax.nn.silu(c_ctx) @ w_ada + b_ada
    sh_l, sc_l, g_l = jnp.split(mod_l[:, None, :], 3, axis=-1)
    sh_c, sc_c, g_c = jnp.split(mod_c, 3)
    hl = rmsnorm(xl, norm_g) * (1.0 + sc_l) + sh_l
    hc = rmsnorm(xc, norm_g) * (1.0 + sc_c) + sh_c
    a_val_l, a_glu_l, a_gate_l, r_x_l, r_gate_l, q_l, k_l, v_l, c_gate_l = split_proj(hl @ w_in)
    a_val_c, a_glu_c, a_gate_c, r_x_c, r_gate_c, q_c, k_c, v_c, c_gate_c = split_proj(hc @ w_in)

    ya_l = conformer_conv(a_val_l, a_glu_l, conv_w, conv_b, ln_g, ln_b, w_pw, b_pw)

    yb_l = jnp.zeros_like(r_x_l)
    yb_c_parts = []
    for d in range(2):
        h0 = jnp.zeros((bsz, W_REC), jnp.float32)
        yc_d, h_ctx = rglru_direction(r_x_c, rconv_w[d], rconv_b[d], w_r[d], b_r[d], w_i[d], b_i[d],
                                      lam[d], h0, reverse=(d == 1))
        yl_d, _ = rglru_direction(r_x_l, rconv_w[d], rconv_b[d], w_r[d], b_r[d], w_i[d], b_i[d],
                                  lam[d], h_ctx, reverse=(d == 1))
        yb_l = yb_l + yl_d
        yb_c_parts.append(yc_d)

    heads = lambda t: t.reshape(bsz, -1, ATT_HEADS, ATT_HEAD_DIM)
    kc_h, vc_h = heads(k_c), heads(v_c)
    yc_l = neighbourhood_attention(heads(q_l), heads(k_l), heads(v_l), kc_h, vc_h, rpb)

    mix_l = jnp.concatenate([ya_l * jax.nn.silu(a_gate_l), yb_l * jax.nn.silu(r_gate_l),
                             yc_l * jax.nn.silu(c_gate_l)], axis=-1)
    xl_new = xl + g_l * (mix_l @ w_out)
    if not need_ctx:
        return xl_new, xc

    ya_c = conformer_conv(a_val_c, a_glu_c, conv_w, conv_b, ln_g, ln_b, w_pw, b_pw)
    yb_c = yb_c_parts[0] + yb_c_parts[1]
    yc_c = context_attention(heads(q_c), kc_h, vc_h)
    mix_c = jnp.concatenate([ya_c * jax.nn.silu(a_gate_c), yb_c * jax.nn.silu(r_gate_c),
                             yc_c * jax.nn.silu(c_gate_c)], axis=-1)
    xc_new = xc + g_c * (mix_c @ w_out)
    return xl_new, xc_new


def setup_inputs(seed: int = 0) -> dict:
    key = jax.random.key(seed)
    ks = jax.random.split(key, 26)
    f32 = jnp.float32
    nrm = lambda k, shape, s: jax.random.normal(k, shape, f32) * s
    d = D_MODEL
    u = jax.random.uniform(ks[21], (DEPTH, 2, W_REC), f32, minval=0.9, maxval=0.999)
    a_base = u ** (1.0 / REC_C)
    lam = jnp.log(a_base) - jnp.log1p(-a_base)
    return {
        "x": nrm(ks[0], (BATCH, SEQ, d), 1.0),
        "c": nrm(ks[1], (BATCH, d), 1.0),
        "ctx": nrm(ks[2], (BATCH, CTX_LEN, d), 1.0),
        "c_ctx": nrm(ks[3], (d,), 1.0),
        "w_ada": nrm(ks[4], (DEPTH, d, 3 * d), 0.3 * d ** -0.5),
        "b_ada": nrm(ks[5], (DEPTH, 3 * d), 0.01),
        "norm_g": 1.0 + nrm(ks[6], (DEPTH, d), 0.02),
        "w_in": nrm(ks[7], (DEPTH, d, N_IN), d ** -0.5),
        "conv_w": nrm(ks[8], (DEPTH, CONV_K, W_CONV), CONV_K ** -0.5),
        "conv_b": nrm(ks[9], (DEPTH, W_CONV), 0.01),
        "ln_g": 1.0 + nrm(ks[10], (DEPTH, W_CONV), 0.02),
        "ln_b": nrm(ks[11], (DEPTH, W_CONV), 0.01),
        "w_pw": nrm(ks[12], (DEPTH, W_CONV, W_CONV), W_CONV ** -0.5),
        "b_pw": nrm(ks[13], (DEPTH, W_CONV), 0.01),
        "rconv_w": nrm(ks[14], (DEPTH, 2, REC_CONV_K, W_REC), REC_CONV_K ** -0.5),
        "rconv_b": nrm(ks[15], (DEPTH, 2, W_REC), 0.01),
        "w_r": nrm(ks[16], (DEPTH, 2, REC_HEADS, REC_BLOCK, REC_BLOCK), REC_BLOCK ** -0.5),
        "b_r": nrm(ks[17], (DEPTH, 2, W_REC), 0.01),
        "w_i": nrm(ks[18], (DEPTH, 2, REC_HEADS, REC_BLOCK, REC_BLOCK), REC_BLOCK ** -0.5),
        "b_i": nrm(ks[19], (DEPTH, 2, W_REC), 0.01),
        "lam": lam,
        "rpb": nrm(ks[20], (DEPTH, ATT_HEADS, 2 * WIN_H - 1, 2 * WIN_W - 1), 0.02),
        "w_out": nrm(ks[22], (DEPTH, D_MIX, d), D_MIX ** -0.5),
        "final_g": 1.0 + nrm(ks[23], (d,), 0.02),
    }


def reference(x, c, ctx, c_ctx, w_ada, b_ada, norm_g, w_in, conv_w, conv_b, ln_g, ln_b, w_pw, b_pw,
              rconv_w, rconv_b, w_r, b_r, w_i, b_i, lam, rpb, w_out, final_g):
    xl, xc = x, ctx
    for l in range(DEPTH):
        xl, xc = hybrid_layer(xl, xc, c, c_ctx, w_ada[l], b_ada[l], norm_g[l], w_in[l], conv_w[l],
                              conv_b[l], ln_g[l], ln_b[l], w_pw[l], b_pw[l], rconv_w[l], rconv_b[l],
                              w_r[l], b_r[l], w_i[l], b_i[l], lam[l], rpb[l], w_out[l],
                              need_ctx=(l < DEPTH - 1))
    return rmsnorm(xl, final_g)
```

```python
import functools

import numpy as np
import jax
import jax.numpy as jnp
from jax import lax
from jax.experimental import pallas as pl
from jax.experimental.pallas import tpu as pltpu

F32 = jnp.float32
BF16 = jnp.bfloat16

GRID_W = 64
WIN_H = 8
WIN_W = 16
CONV_K = 31
REC_CONV_K = 4
REC_HEADS = 8
REC_C = 8.0
HEAD_DIM = 128
EPS = 1e-6
NEG = -1e30

ROW_TILE = 256
HALO = 16
ATT_ROWS = 4
ATT_HEADS_PER_STEP = 4
VMEM_LIMIT = 56 * 1024 * 1024


def _cparams(sem, vmem=None):
    return pltpu.CompilerParams(dimension_semantics=sem, vmem_limit_bytes=vmem)


def _sigmoid(x):
    return 1.0 / (1.0 + jnp.exp(-x))


def _silu(x):
    return x * _sigmoid(x)


def _ada_kernel(cc_ref, w_ref, b_ref, o_ref):
    a = _silu(cc_ref[...]).astype(BF16)
    w = w_ref[0].astype(BF16)
    o_ref[0] = jnp.dot(a, w, preferred_element_type=F32) + b_ref[0]


def _ada_all_layers(cc, w_ada, b_ada):
    depth, d, n = w_ada.shape
    tn = 512
    return pl.pallas_call(
        _ada_kernel,
        out_shape=jax.ShapeDtypeStruct((depth, 8, n), F32),
        grid=(depth, n // tn),
        in_specs=[pl.BlockSpec((8, d), lambda l, j: (0, 0)),
                  pl.BlockSpec((1, d, tn), lambda l, j: (l, 0, j)),
                  pl.BlockSpec((1, 1, tn), lambda l, j: (l, 0, j))],
        out_specs=pl.BlockSpec((1, 8, tn), lambda l, j: (l, 0, j)),
        compiler_params=_cparams(("arbitrary", "arbitrary"), VMEM_LIMIT),
        name="ada_mod",
    )(cc, w_ada, b_ada.reshape(depth, 1, n))


def _norm_mod_kernel(x_ref, g_ref, mod_ref, o_ref, *, d):
    x = x_ref[...]
    y = x * lax.rsqrt(jnp.mean(x * x, axis=-1, keepdims=True) + EPS) * g_ref[...]
    mod = mod_ref[0]
    o_ref[...] = (y * (1.0 + mod[:, d:2 * d]) + mod[:, :d]).astype(o_ref.dtype)


def _norm_mod(x_all, g, mod3):
    sa, d = x_all.shape
    return pl.pallas_call(
        functools.partial(_norm_mod_kernel, d=d),
        out_shape=jax.ShapeDtypeStruct((sa, d), BF16),
        grid=(sa // ROW_TILE,),
        in_specs=[pl.BlockSpec((ROW_TILE, d), lambda i: (i, 0)),
                  pl.BlockSpec((1, d), lambda i: (0, 0)),
                  pl.BlockSpec((1, 1, 3 * d), lambda i: (jnp.where(i == 0, 1, 0), 0, 0))],
        out_specs=pl.BlockSpec((ROW_TILE, d), lambda i: (i, 0)),
        compiler_params=_cparams(("arbitrary",), VMEM_LIMIT),
        name="norm_mod",
    )(x_all, g.reshape(1, d), mod3)


def _final_norm_kernel(x_ref, g_ref, o_ref):
    x = x_ref[...]
    o_ref[...] = x * lax.rsqrt(jnp.mean(x * x, axis=-1, keepdims=True) + EPS) * g_ref[...]


def _final_norm(x_all, g, n_ctx_tiles):
    sa, d = x_all.shape
    s = sa - n_ctx_tiles * ROW_TILE
    return pl.pallas_call(
        _final_norm_kernel,
        out_shape=jax.ShapeDtypeStruct((s, d), F32),
        grid=(s // ROW_TILE,),
        in_specs=[pl.BlockSpec((ROW_TILE, d), lambda i: (i + n_ctx_tiles, 0)),
                  pl.BlockSpec((1, d), lambda i: (0, 0))],
        out_specs=pl.BlockSpec((ROW_TILE, d), lambda i: (i, 0)),
        compiler_params=_cparams(("arbitrary",), VMEM_LIMIT),
        name="final_norm",
    )(x_all, g.reshape(1, d))


def _matmul_kernel(a_ref, b_ref, o_ref):
    o_ref[...] = jnp.dot(a_ref[...], b_ref[...], preferred_element_type=F32).astype(o_ref.dtype)


def _row_tile(sa, cap):
    best = ROW_TILE
    for t in range(ROW_TILE, cap + 1, ROW_TILE):
        if sa % t == 0:
            best = t
    return best


def _in_proj(h, w):
    sa, d = h.shape
    n = w.shape[1]
    tm, tn = _row_tile(sa, 1280), 1024
    return pl.pallas_call(
        _matmul_kernel,
        out_shape=jax.ShapeDtypeStruct((sa, n), BF16),
        grid=(sa // tm, n // tn),
        in_specs=[pl.BlockSpec((tm, d), lambda i, j: (i, 0)),
                  pl.BlockSpec((d, tn), lambda i, j: (0, j))],
        out_specs=pl.BlockSpec((tm, tn), lambda i, j: (i, j)),
        compiler_params=_cparams(("arbitrary", "arbitrary"), VMEM_LIMIT),
        name="in_proj",
    )(h, w)


def _conv_kernel(val_ref, glu_ref, gate_ref, vb_ref, gb_ref, va_ref, ga_ref,
                 cw_ref, cb_ref, lng_ref, lnb_ref, wpw_ref, bpw_ref, o_ref, u_scr, *, t):
    i = pl.program_id(0)
    n = pl.num_programs(0)

    def glu(v_ref, g_ref):
        return v_ref[...].astype(F32) * _sigmoid(g_ref[...].astype(F32))

    has_before = jnp.where(i >= 2, 1.0, 0.0)
    has_after = jnp.where((i >= 1) & (i < n - 1), 1.0, 0.0)
    u_scr[0:HALO, :] = glu(vb_ref, gb_ref) * has_before
    u_scr[HALO:HALO + t, :] = glu(val_ref, glu_ref)
    u_scr[HALO + t:HALO + t + HALO, :] = glu(va_ref, ga_ref) * has_after

    pad = CONV_K // 2
    acc = jnp.zeros((t, u_scr.shape[1]), F32) + cb_ref[...]
    for k in range(CONV_K):
        acc = acc + cw_ref[k:k + 1, :] * u_scr[pl.ds(HALO - pad + k, t), :]

    mu = jnp.mean(acc, axis=-1, keepdims=True)
    cen = acc - mu
    var = jnp.mean(cen * cen, axis=-1, keepdims=True)
    y = _silu(cen * lax.rsqrt(var + EPS) * lng_ref[...] + lnb_ref[...])
    out = jnp.dot(y.astype(BF16), wpw_ref[...], preferred_element_type=F32) + bpw_ref[...]
    o_ref[...] = (out * _silu(gate_ref[...].astype(F32))).astype(o_ref.dtype)


def _conv_mixer(z, cw, cb, lng, lnb, wpw, bpw):
    sa = z.shape[0]
    c = cw.shape[1]
    t = ROW_TILE
    hb = t // HALO
    last_h = sa // HALO - 1
    cw_p = jnp.zeros((32, c), F32).at[:CONV_K].set(cw)
    row = lambda v: v.reshape(1, c)
    before = lambda col: pl.BlockSpec((HALO, c), lambda i: (jnp.maximum(i * hb - 1, 0), col))
    after = lambda col: pl.BlockSpec((HALO, c), lambda i: (jnp.minimum((i + 1) * hb, last_h), col))
    full = lambda shape: pl.BlockSpec(shape, lambda i: (0, 0))
    return pl.pallas_call(
        functools.partial(_conv_kernel, t=t),
        out_shape=jax.ShapeDtypeStruct((sa, c), BF16),
        grid=(sa // t,),
        in_specs=[pl.BlockSpec((t, c), lambda i: (i, 0)),
                  pl.BlockSpec((t, c), lambda i: (i, 1)),
                  pl.BlockSpec((t, c), lambda i: (i, 2)),
                  before(0), before(1), after(0), after(1),
                  full((32, c)), full((1, c)), full((1, c)), full((1, c)),
                  full((c, c)), full((1, c))],
        out_specs=pl.BlockSpec((t, c), lambda i: (i, 0)),
        scratch_shapes=[pltpu.VMEM((t + 2 * HALO, c), F32)],
        compiler_params=_cparams(("arbitrary",), VMEM_LIMIT),
        name="conv_mixer",
    )(z, z, z, z, z, z, z, cw_p, row(cb), row(lng), row(lnb), wpw, row(bpw))


def _scan_tile(a, b, reverse):
    t = a.shape[0]
    rows = lax.broadcasted_iota(jnp.int32, a.shape, 0)
    s = 1
    while s < t:
        if s < 8:
            if reverse:
                keep = rows < t - s
                a_sh = jnp.where(keep, pltpu.roll(a, t - s, 0), 1.0)
                b_sh = jnp.where(keep, pltpu.roll(b, t - s, 0), 0.0)
            else:
                keep = rows >= s
                a_sh = jnp.where(keep, pltpu.roll(a, s, 0), 1.0)
                b_sh = jnp.where(keep, pltpu.roll(b, s, 0), 0.0)
        else:
            ones = jnp.ones((s, a.shape[1]), F32)
            zeros = jnp.zeros((s, a.shape[1]), F32)
            if reverse:
                a_sh = jnp.concatenate([a[s:], ones], axis=0)
                b_sh = jnp.concatenate([b[s:], zeros], axis=0)
            else:
                a_sh = jnp.concatenate([ones, a[:t - s]], axis=0)
                b_sh = jnp.concatenate([zeros, b[:t - s]], axis=0)
        b = a * b_sh + b
        a = a * a_sh
        s *= 2
    return a, b


def _rglru_kernel(*refs, t, reverse):
    if reverse:
        (x_ref, xh_ref, hf_ref, gate_ref, cw_ref, cb_ref, wri_ref, br_ref, bi_ref, lam_ref,
         o_ref, carry_scr, xs_scr) = refs
    else:
        (x_ref, xh_ref, cw_ref, cb_ref, wri_ref, br_ref, bi_ref, lam_ref,
         o_ref, carry_scr, xs_scr) = refs
    s = pl.program_id(0)

    @pl.when(s == 0)
    def _():
        carry_scr[...] = jnp.zeros_like(carry_scr)

    c = x_ref.shape[1]
    has_halo = jnp.where(s >= 2, 1.0, 0.0)
    x = x_ref[...].astype(F32)
    halo = xh_ref[...].astype(F32) * has_halo
    if reverse:
        xs_scr[0:t, :] = x
        xs_scr[t:t + HALO, :] = halo
        base = 0
    else:
        xs_scr[0:HALO, :] = halo
        xs_scr[HALO:HALO + t, :] = x
        base = HALO - (REC_CONV_K - 1)
    xc = jnp.zeros((t, c), F32) + cb_ref[...]
    for k in range(REC_CONV_K):
        xc = xc + cw_ref[k:k + 1, :] * xs_scr[pl.ds(base + k, t), :]

    xcb = xc.astype(BF16)
    blk = c // REC_HEADS
    r_parts, i_parts = [], []
    for hh in range(REC_HEADS):
        ri = jnp.dot(xcb[:, hh * blk:(hh + 1) * blk], wri_ref[hh], preferred_element_type=F32)
        r_parts.append(ri[:, :blk])
        i_parts.append(ri[:, blk:])
    r = _sigmoid(jnp.concatenate(r_parts, axis=1) + br_ref[...])
    ig = _sigmoid(jnp.concatenate(i_parts, axis=1) + bi_ref[...])

    lam = lam_ref[...]
    softplus_neg_lam = jnp.maximum(-lam, 0.0) + jnp.log(1.0 + jnp.exp(-jnp.abs(lam)))
    a = jnp.exp(-REC_C * r * softplus_neg_lam)
    b = jnp.sqrt(1.0 - a * a) * (ig * xc)

    a_cum, b_cum = _scan_tile(a, b, reverse)
    h = a_cum * carry_scr[0:1, :] + b_cum
    carry_scr[0:1, :] = h[0:1, :] if reverse else h[t - 1:t, :]

    if reverse:
        y = hf_ref[...].astype(F32) + h
        o_ref[...] = (y * _silu(gate_ref[...].astype(F32))).astype(o_ref.dtype)
    else:
        o_ref[...] = h.astype(o_ref.dtype)


def _rglru(z, hf, cw, cb, w_r, w_i, b_r, b_i, lam, reverse):
    sa = z.shape[0]
    c = cw.shape[1]
    t = ROW_TILE
    n = sa // t
    hb = t // HALO
    last_h = sa // HALO - 1
    wri = jnp.concatenate([w_r, w_i], axis=-1).astype(BF16)
    cw_p = jnp.zeros((8, c), F32).at[:REC_CONV_K].set(cw)
    row = lambda v: v.reshape(1, c)
    full = lambda shape: pl.BlockSpec(shape, lambda s: (0,) * len(shape))
    if reverse:
        tile = lambda s: jnp.where(s == 0, 0, n - s)
        halo_spec = pl.BlockSpec((HALO, c), lambda s: (jnp.minimum((tile(s) + 1) * hb, last_h), 3))
    else:
        tile = lambda s: s
        halo_spec = pl.BlockSpec((HALO, c), lambda s: (jnp.maximum(tile(s) * hb - 1, 0), 3))
    in_specs = [pl.BlockSpec((t, c), lambda s: (tile(s), 3)), halo_spec]
    args = [z, z]
    if reverse:
        in_specs += [pl.BlockSpec((t, c), lambda s: (tile(s), 0)),
                     pl.BlockSpec((t, c), lambda s: (tile(s), 4))]
        args += [hf, z]
    in_specs += [full((8, c)), full((1, c)), full(wri.shape), full((1, c)), full((1, c)), full((1, c))]
    args += [cw_p, row(cb), wri, row(b_r), row(b_i), row(lam)]
    return pl.pallas_call(
        functools.partial(_rglru_kernel, t=t, reverse=reverse),
        out_shape=jax.ShapeDtypeStruct((sa, c), BF16),
        grid=(n,),
        in_specs=in_specs,
        out_specs=pl.BlockSpec((t, c), lambda s: (tile(s), 0)),
        scratch_shapes=[pltpu.VMEM((8, c), F32), pltpu.VMEM((t + HALO, c), F32)],
        compiler_params=_cparams(("arbitrary",), VMEM_LIMIT),
        name="rglru_bwd" if reverse else "rglru_fwd",
    )(*args)


def _attn_tables(rpb, rows):
    r4, w = ATT_ROWS, GRID_W
    n_blocks = rows // r4
    assert rows % r4 == 0 and rows >= 3 * r4 and rows >= WIN_H
    cq = np.arange(w)[:, None]
    ck = np.arange(w)[None, :]
    cs = np.clip(cq - WIN_W // 2, 0, w - WIN_W)
    col_ok = (ck >= cs) & (ck < cs + WIN_W)
    dcol = np.clip(ck - cq + (WIN_W - 1), 0, 2 * WIN_W - 2)
    t2 = jnp.take(rpb, jnp.asarray(dcol.reshape(-1)), axis=2).reshape(rpb.shape[0], 2 * WIN_H - 1, w, w)
    per_q = [t2[:, (WIN_H - 1) - r4 - rq:(WIN_H - 1) - r4 - rq + 3 * r4] for rq in range(r4)]
    tbl = jnp.stack(per_q, axis=1)
    tbl = tbl.transpose(0, 1, 3, 2, 4).reshape(rpb.shape[0], r4 * w, 3 * r4 * w)

    def mask_for(b):
        rq = np.arange(r4)[:, None]
        jr = np.arange(3 * r4)[None, :]
        r = r4 * b + rq
        rk = r4 * b - r4 + jr
        rs = np.clip(r - WIN_H // 2, 0, rows - WIN_H)
        row_ok = (rk >= rs) & (rk < rs + WIN_H) & (rk >= 0) & (rk < rows)
        m = row_ok[:, None, :, None] & col_ok[None, :, None, :]
        return m.reshape(r4 * w, 3 * r4 * w)

    masks = np.stack([mask_for(0), mask_for(1), mask_for(n_blocks - 1), np.zeros_like(mask_for(0))])
    return jnp.where(jnp.asarray(masks)[:, None], tbl[None], NEG)


def _attn_kernel(q_ref, k0_ref, k1_ref, k2_ref, v0_ref, v1_ref, v2_ref, kc_ref, vc_ref, g_ref,
                 t_ref, o_ref, *, heads, scale):
    nt = (((1,), (1,)), ((), ()))
    for h in range(heads):
        sl = slice(h * HEAD_DIM, (h + 1) * HEAD_DIM)
        q = (q_ref[:, sl].astype(F32) * scale).astype(BF16)
        k = jnp.concatenate([k0_ref[:, sl], k1_ref[:, sl], k2_ref[:, sl]], axis=0)
        v = jnp.concatenate([v0_ref[:, sl], v1_ref[:, sl], v2_ref[:, sl]], axis=0)
        s_loc = lax.dot_general(q, k, nt, preferred_element_type=F32) + t_ref[0, h]
        s_ctx = lax.dot_general(q, kc_ref[:, sl], nt, preferred_element_type=F32)
        m = jnp.maximum(jnp.max(s_loc, axis=-1, keepdims=True), jnp.max(s_ctx, axis=-1, keepdims=True))
        p_loc = jnp.exp(s_loc - m)
        p_ctx = jnp.exp(s_ctx - m)
        denom = jnp.sum(p_loc, axis=-1, keepdims=True) + jnp.sum(p_ctx, axis=-1, keepdims=True)
        o = (jnp.dot(p_loc.astype(BF16), v, preferred_element_type=F32)
             + jnp.dot(p_ctx.astype(BF16), vc_ref[:, sl], preferred_element_type=F32))
        o = o / denom
        o_ref[:, sl] = (o * _silu(g_ref[:, sl].astype(F32))).astype(o_ref.dtype)


def _attention(z, tables, w_att, col0):
    sa = z.shape[0]
    t = ROW_TILE
    nb = sa // t - 1
    hg = ATT_HEADS_PER_STEP
    wl = hg * HEAD_DIM
    n_heads = w_att // HEAD_DIM
    cq, ckk, cv, cg = (c // wl for c in col0)
    clampb = lambda s: jnp.clip(s, 1, nb)
    variant = lambda s: jnp.where(s == 0, 3, jnp.where(s == 1, 0, jnp.where(s == nb, 2, 1)))
    blk = lambda rowf, colb: pl.BlockSpec((t, wl), lambda g, s: (rowf(s), colb + g))
    return pl.pallas_call(
        functools.partial(_attn_kernel, heads=hg, scale=HEAD_DIM ** -0.5),
        out_shape=jax.ShapeDtypeStruct((sa, w_att), BF16),
        grid=(n_heads // hg, nb + 1),
        in_specs=[blk(lambda s: s, cq),
                  blk(lambda s: clampb(s - 1), ckk), blk(lambda s: clampb(s), ckk), blk(lambda s: clampb(s + 1), ckk),
                  blk(lambda s: clampb(s - 1), cv), blk(lambda s: clampb(s), cv), blk(lambda s: clampb(s + 1), cv),
                  blk(lambda s: 0, ckk), blk(lambda s: 0, cv),
                  blk(lambda s: s, cg),
                  pl.BlockSpec((1, hg, t, 3 * t), lambda g, s: (variant(s), g, 0, 0))],
        out_specs=pl.BlockSpec((t, wl), lambda g, s: (s, g)),
        compiler_params=_cparams(("arbitrary", "arbitrary"), VMEM_LIMIT),
        name="nbr_attention",
    )(z, z, z, z, z, z, z, z, z, z, tables)


def _out_proj_kernel(a_ref, b_ref, c_ref, wa_ref, wb_ref, wc_ref, x_ref, g_ref, o_ref, *, tm, n_ctx):
    i = pl.program_id(1)
    acc = (jnp.dot(a_ref[...], wa_ref[...], preferred_element_type=F32)
           + jnp.dot(b_ref[...], wb_ref[...], preferred_element_type=F32)
           + jnp.dot(c_ref[...], wc_ref[...], preferred_element_type=F32))
    rows = i * tm + lax.broadcasted_iota(jnp.int32, acc.shape, 0)
    g = jnp.where(rows < n_ctx, g_ref[1:2, :], g_ref[0:1, :])
    o_ref[...] = x_ref[...] + g * acc


def _out_proj(mix_a, mix_b, mix_c, w_out, x_all, mod, n_ctx):
    sa, d = x_all.shape
    wa, wb, wc = mix_a.shape[1], mix_b.shape[1], mix_c.shape[1]
    assert wa == wb and wc % wa == 0
    tm, tn = _row_tile(sa, 640), 1024
    gate_blk0 = 2 * d // tn
    return pl.pallas_call(
        functools.partial(_out_proj_kernel, tm=tm, n_ctx=n_ctx),
        out_shape=jax.ShapeDtypeStruct((sa, d), F32),
        grid=(d // tn, sa // tm),
        in_specs=[pl.BlockSpec((tm, wa), lambda j, i: (i, 0)),
                  pl.BlockSpec((tm, wb), lambda j, i: (i, 0)),
                  pl.BlockSpec((tm, wc), lambda j, i: (i, 0)),
                  pl.BlockSpec((wa, tn), lambda j, i: (0, j)),
                  pl.BlockSpec((wb, tn), lambda j, i: (1, j)),
                  pl.BlockSpec((wc, tn), lambda j, i: ((wa + wb) // wc, j)),
                  pl.BlockSpec((tm, tn), lambda j, i: (i, j)),
                  pl.BlockSpec((8, tn), lambda j, i: (0, gate_blk0 + j))],
        out_specs=pl.BlockSpec((tm, tn), lambda j, i: (i, j)),
        compiler_params=_cparams(("arbitrary", "arbitrary"), VMEM_LIMIT),
        name="out_proj",
    )(mix_a, mix_b, mix_c, w_out, w_out, w_out, x_all, mod)


def kernel(x, c, ctx, c_ctx, w_ada, b_ada, norm_g, w_in, conv_w, conv_b, ln_g, ln_b, w_pw, b_pw,
           rconv_w, rconv_b, w_r, b_r, w_i, b_i, lam, rpb, w_out, final_g):
    bsz, s, d = x.shape
    n_ctx = ctx.shape[1]
    depth = w_ada.shape[0]
    w_conv = conv_w.shape[2]
    w_rec = rconv_w.shape[3]
    w_att = rpb.shape[1] * HEAD_DIM
    assert bsz == 1 and n_ctx == ROW_TILE and s % ROW_TILE == 0 and ATT_ROWS * GRID_W == ROW_TILE
    assert w_conv == w_rec == 1024
    rows = s // GRID_W
    col_q = 3 * w_conv + 2 * w_rec
    col0 = (col_q, col_q + w_att, col_q + 2 * w_att, col_q + 3 * w_att)

    cc = jnp.zeros((8, d), F32).at[0].set(c[0]).at[1].set(c_ctx)
    mods = _ada_all_layers(cc, w_ada, b_ada)
    x_all = jnp.concatenate([ctx[0], x[0]], axis=0)

    for l in range(depth):
        mod = mods[l]
        h = _norm_mod(x_all, norm_g[l], mod.reshape(8, 1, 3 * d))
        z = _in_proj(h, w_in[l].astype(BF16))
        mix_a = _conv_mixer(z, conv_w[l], conv_b[l], ln_g[l], ln_b[l], w_pw[l].astype(BF16), b_pw[l])
        hf = _rglru(z, None, rconv_w[l, 0], rconv_b[l, 0], w_r[l, 0], w_i[l, 0], b_r[l, 0], b_i[l, 0],
                    lam[l, 0], reverse=False)
        mix_b = _rglru(z, hf, rconv_w[l, 1], rconv_b[l, 1], w_r[l, 1], w_i[l, 1], b_r[l, 1], b_i[l, 1],
                       lam[l, 1], reverse=True)
        mix_c = _attention(z, _attn_tables(rpb[l], rows), w_att, col0)
        x_all = _out_proj(mix_a, mix_b, mix_c, w_out[l].astype(BF16), x_all, mod, n_ctx)

    return _final_norm(x_all, final_g, n_ctx // ROW_TILE)[None]
```

```python
import functools

import numpy as np
import jax
import jax.numpy as jnp
from jax import lax
from jax.experimental import pallas as pl
from jax.experimental.pallas import tpu as pltpu

F32 = jnp.float32
BF16 = jnp.bfloat16

GRID_W = 64
WIN_H = 8
WIN_W = 16
CONV_K = 31
REC_CONV_K = 4
REC_HEADS = 8
REC_C = 8.0
HEAD_DIM = 128
EPS = 1e-6
NEG = -1e30
LOG2E = 1.4426950408889634

ROW_TILE = 256
HALO = 16
ATT_ROWS = 4
ATT_HEADS_PER_STEP = 8
VMEM_LIMIT = 56 * 1024 * 1024


def _cparams(sem, vmem=None):
    return pltpu.CompilerParams(dimension_semantics=sem, vmem_limit_bytes=vmem)


def _sigmoid(x):
    return 1.0 / (1.0 + jnp.exp2(x * -LOG2E))


def _silu(x):
    return x * _sigmoid(x)


def _ada_kernel(cc_ref, w_ref, b_ref, o_ref):
    a = _silu(cc_ref[...]).astype(BF16)
    w = w_ref[0].astype(BF16)
    o_ref[0] = jnp.dot(a, w, preferred_element_type=F32) + b_ref[0]


def _ada_all_layers(cc, w_ada, b_ada):
    depth, d, n = w_ada.shape
    tn = 512
    return pl.pallas_call(
        _ada_kernel,
        out_shape=jax.ShapeDtypeStruct((depth, 8, n), F32),
        grid=(depth, n // tn),
        in_specs=[pl.BlockSpec((8, d), lambda l, j: (0, 0)),
                  pl.BlockSpec((1, d, tn), lambda l, j: (l, 0, j)),
                  pl.BlockSpec((1, 1, tn), lambda l, j: (l, 0, j))],
        out_specs=pl.BlockSpec((1, 8, tn), lambda l, j: (l, 0, j)),
        compiler_params=_cparams(("arbitrary", "arbitrary"), VMEM_LIMIT),
        name="ada_mod",
    )(cc, w_ada, b_ada.reshape(depth, 1, n))


def _norm_mod_kernel(x_ref, g_ref, mod_ref, o_ref, *, d):
    x = x_ref[...]
    y = x * lax.rsqrt(jnp.mean(x * x, axis=-1, keepdims=True) + EPS) * g_ref[...]
    mod = mod_ref[0]
    o_ref[...] = (y * (1.0 + mod[:, d:2 * d]) + mod[:, :d]).astype(o_ref.dtype)


def _norm_mod(x_all, g, mod3):
    sa, d = x_all.shape
    return pl.pallas_call(
        functools.partial(_norm_mod_kernel, d=d),
        out_shape=jax.ShapeDtypeStruct((sa, d), BF16),
        grid=(sa // ROW_TILE,),
        in_specs=[pl.BlockSpec((ROW_TILE, d), lambda i: (i, 0)),
                  pl.BlockSpec((1, d), lambda i: (0, 0)),
                  pl.BlockSpec((1, 1, 3 * d), lambda i: (jnp.where(i == 0, 1, 0), 0, 0))],
        out_specs=pl.BlockSpec((ROW_TILE, d), lambda i: (i, 0)),
        compiler_params=_cparams(("arbitrary",), VMEM_LIMIT),
        name="norm_mod",
    )(x_all, g.reshape(1, d), mod3)


def _final_norm_kernel(x_ref, g_ref, o_ref):
    x = x_ref[...]
    o_ref[...] = x * lax.rsqrt(jnp.mean(x * x, axis=-1, keepdims=True) + EPS) * g_ref[...]


def _final_norm(x_all, g, n_ctx_tiles):
    sa, d = x_all.shape
    s = sa - n_ctx_tiles * ROW_TILE
    return pl.pallas_call(
        _final_norm_kernel,
        out_shape=jax.ShapeDtypeStruct((s, d), F32),
        grid=(s // ROW_TILE,),
        in_specs=[pl.BlockSpec((ROW_TILE, d), lambda i: (i + n_ctx_tiles, 0)),
                  pl.BlockSpec((1, d), lambda i: (0, 0))],
        out_specs=pl.BlockSpec((ROW_TILE, d), lambda i: (i, 0)),
        compiler_params=_cparams(("arbitrary",), VMEM_LIMIT),
        name="final_norm",
    )(x_all, g.reshape(1, d))


def _matmul_kernel(a_ref, b_ref, o_ref):
    o_ref[...] = jnp.dot(a_ref[...], b_ref[0], preferred_element_type=F32).astype(o_ref.dtype)


def _row_tile(sa, cap):
    best = ROW_TILE
    for t in range(ROW_TILE, cap + 1, ROW_TILE):
        if sa % t == 0:
            best = t
    return best


def _in_proj(h, w_all, l):
    sa, d = h.shape
    n = w_all.shape[2]
    tm, tn = _row_tile(sa, 1280), 1024
    return pl.pallas_call(
        _matmul_kernel,
        out_shape=jax.ShapeDtypeStruct((sa, n), BF16),
        grid=(sa // tm, n // tn),
        in_specs=[pl.BlockSpec((tm, d), lambda i, j: (i, 0)),
                  pl.BlockSpec((1, d, tn), lambda i, j: (l, 0, j))],
        out_specs=pl.BlockSpec((tm, tn), lambda i, j: (i, j)),
        compiler_params=_cparams(("arbitrary", "arbitrary"), VMEM_LIMIT),
        name="in_proj",
    )(h, w_all)


def _conv_kernel(val_ref, glu_ref, gate_ref, vb_ref, gb_ref, va_ref, ga_ref,
                 cw_ref, cb_ref, lng_ref, lnb_ref, wpw_ref, bpw_ref, o_ref, u_scr, *, t):
    i = pl.program_id(0)
    n = pl.num_programs(0)

    def glu(v_ref, g_ref):
        return v_ref[...].astype(F32) * _sigmoid(g_ref[...].astype(F32))

    has_before = jnp.where(i >= 2, 1.0, 0.0)
    has_after = jnp.where((i >= 1) & (i < n - 1), 1.0, 0.0)
    u_scr[0:HALO, :] = glu(vb_ref, gb_ref) * has_before
    u_scr[HALO:HALO + t, :] = glu(val_ref, glu_ref)
    u_scr[HALO + t:HALO + t + HALO, :] = glu(va_ref, ga_ref) * has_after

    first = HALO - CONV_K // 2
    acc = None
    for res in range(8):
        g = None
        for off in range(res, first + CONV_K, 8):
            if off < first:
                continue
            term = cw_ref[off - first:off - first + 1, :] * u_scr[pl.ds(off - res, t + 8), :]
            g = term if g is None else g + term
        if res:
            g = pltpu.roll(g, t + 8 - res, 0)
        acc = g[:t] if acc is None else acc + g[:t]
    acc = acc + cb_ref[...]

    mu = jnp.mean(acc, axis=-1, keepdims=True)
    cen = acc - mu
    var = jnp.mean(cen * cen, axis=-1, keepdims=True)
    y = _silu(cen * lax.rsqrt(var + EPS) * lng_ref[...] + lnb_ref[...])
    out = jnp.dot(y.astype(BF16), wpw_ref[...], preferred_element_type=F32) + bpw_ref[...]
    o_ref[...] = (out * _silu(gate_ref[...].astype(F32))).astype(o_ref.dtype)


def _conv_mixer(z, cw, cb, lng, lnb, wpw, bpw):
    sa = z.shape[0]
    c = cw.shape[1]
    t = ROW_TILE
    hb = t // HALO
    last_h = sa // HALO - 1
    cw_p = jnp.zeros((32, c), F32).at[:CONV_K].set(cw)
    row = lambda v: v.reshape(1, c)
    before = lambda col: pl.BlockSpec((HALO, c), lambda i: (jnp.maximum(i * hb - 1, 0), col))
    after = lambda col: pl.BlockSpec((HALO, c), lambda i: (jnp.minimum((i + 1) * hb, last_h), col))
    full = lambda shape: pl.BlockSpec(shape, lambda i: (0, 0))
    return pl.pallas_call(
        functools.partial(_conv_kernel, t=t),
        out_shape=jax.ShapeDtypeStruct((sa, c), BF16),
        grid=(sa // t,),
        in_specs=[pl.BlockSpec((t, c), lambda i: (i, 0)),
                  pl.BlockSpec((t, c), lambda i: (i, 1)),
                  pl.BlockSpec((t, c), lambda i: (i, 2)),
                  before(0), before(1), after(0), after(1),
                  full((32, c)), full((1, c)), full((1, c)), full((1, c)),
                  full((c, c)), full((1, c))],
        out_specs=pl.BlockSpec((t, c), lambda i: (i, 0)),
        scratch_shapes=[pltpu.VMEM((t + 2 * HALO, c), F32)],
        compiler_params=_cparams(("arbitrary",), VMEM_LIMIT),
        name="conv_mixer",
    )(z, z, z, z, z, z, z, cw_p, row(cb), row(lng), row(lnb), wpw, row(bpw))


SUB = 8


def _scan_tile(a, b, h_in, reverse):
    t, c = a.shape
    groups = t // SUB
    a = a.reshape(groups, SUB, c)
    b = b.reshape(groups, SUB, c)
    rows = lax.broadcasted_iota(jnp.int32, a.shape, 1)
    s = 1
    while s < SUB:
        if reverse:
            keep = rows < SUB - s
            shift = SUB - s
        else:
            keep = rows >= s
            shift = s
        a_sh = jnp.where(keep, pltpu.roll(a, shift, 1), 1.0)
        b_sh = jnp.where(keep, pltpu.roll(b, shift, 1), 0.0)
        b = a * b_sh + b
        a = a * a_sh
        s *= 2
    out = [None] * groups
    h = h_in
    for g in (range(groups - 1, -1, -1) if reverse else range(groups)):
        hg = a[g] * h + b[g]
        out[g] = hg
        h = hg[0:1, :] if reverse else hg[SUB - 1:SUB, :]
    return jnp.concatenate(out, axis=0), h


def _rglru_kernel(*refs, t, reverse):
    if reverse:
        (x_ref, xh_ref, hf_ref, gate_ref, cw_ref, cb_ref, wri_ref, br_ref, bi_ref, lam_ref,
         o_ref, carry_scr, xs_scr) = refs
    else:
        (x_ref, xh_ref, cw_ref, cb_ref, wri_ref, br_ref, bi_ref, lam_ref,
         o_ref, carry_scr, xs_scr) = refs
    s = pl.program_id(0)

    @pl.when(s == 0)
    def _():
        carry_scr[...] = jnp.zeros_like(carry_scr)

    c = x_ref.shape[1]
    has_halo = jnp.where(s >= 2, 1.0, 0.0)
    x = x_ref[...].astype(F32)
    halo = xh_ref[...].astype(F32) * has_halo
    if reverse:
        xs_scr[0:t, :] = x
        xs_scr[t:t + HALO, :] = halo
        base = 0
    else:
        xs_scr[0:HALO, :] = halo
        xs_scr[HALO:HALO + t, :] = x
        base = HALO - (REC_CONV_K - 1)
    xc = jnp.zeros((t, c), F32) + cb_ref[...]
    for k in range(REC_CONV_K):
        xc = xc + cw_ref[k:k + 1, :] * xs_scr[pl.ds(base + k, t), :]

    xcb = xc.astype(BF16)
    blk = c // REC_HEADS
    r_parts, i_parts = [], []
    for hh in range(REC_HEADS):
        ri = jnp.dot(xcb[:, hh * blk:(hh + 1) * blk], wri_ref[hh], preferred_element_type=F32)
        r_parts.append(ri[:, :blk])
        i_parts.append(ri[:, blk:])
    r = _sigmoid(jnp.concatenate(r_parts, axis=1) + br_ref[...])
    ig = _sigmoid(jnp.concatenate(i_parts, axis=1) + bi_ref[...])

    lam = lam_ref[...]
    softplus_neg_lam = jnp.maximum(-lam, 0.0) + jnp.log(1.0 + jnp.exp(-jnp.abs(lam)))
    a = jnp.exp2(r * (-REC_C * LOG2E * softplus_neg_lam))
    one_m_a2 = 1.0 - a * a
    b = (one_m_a2 * lax.rsqrt(jnp.maximum(one_m_a2, 1e-30))) * (ig * xc)

    h, h_out = _scan_tile(a, b, carry_scr[0:1, :], reverse)
    carry_scr[0:1, :] = h_out

    if reverse:
        y = hf_ref[...].astype(F32) + h
        o_ref[...] = (y * _silu(gate_ref[...].astype(F32))).astype(o_ref.dtype)
    else:
        o_ref[...] = h.astype(o_ref.dtype)


def _rglru(z, hf, cw, cb, w_r, w_i, b_r, b_i, lam, reverse):
    sa = z.shape[0]
    c = cw.shape[1]
    t = ROW_TILE
    n = sa // t
    hb = t // HALO
    last_h = sa // HALO - 1
    wri = jnp.concatenate([w_r, w_i], axis=-1).astype(BF16)
    cw_p = jnp.zeros((8, c), F32).at[:REC_CONV_K].set(cw)
    row = lambda v: v.reshape(1, c)
    full = lambda shape: pl.BlockSpec(shape, lambda s: (0,) * len(shape))
    if reverse:
        tile = lambda s: jnp.where(s == 0, 0, n - s)
        halo_spec = pl.BlockSpec((HALO, c), lambda s: (jnp.minimum((tile(s) + 1) * hb, last_h), 3))
    else:
        tile = lambda s: s
        halo_spec = pl.BlockSpec((HALO, c), lambda s: (jnp.maximum(tile(s) * hb - 1, 0), 3))
    in_specs = [pl.BlockSpec((t, c), lambda s: (tile(s), 3)), halo_spec]
    args = [z, z]
    if reverse:
        in_specs += [pl.BlockSpec((t, c), lambda s: (tile(s), 0)),
                     pl.BlockSpec((t, c), lambda s: (tile(s), 4))]
        args += [hf, z]
    in_specs += [full((8, c)), full((1, c)), full(wri.shape), full((1, c)), full((1, c)), full((1, c))]
    args += [cw_p, row(cb), wri, row(b_r), row(b_i), row(lam)]
    return pl.pallas_call(
        functools.partial(_rglru_kernel, t=t, reverse=reverse),
        out_shape=jax.ShapeDtypeStruct((sa, c), BF16),
        grid=(n,),
        in_specs=in_specs,
        out_specs=pl.BlockSpec((t, c), lambda s: (tile(s), 0)),
        scratch_shapes=[pltpu.VMEM((8, c), F32), pltpu.VMEM((t + HALO, c), F32)],
        compiler_params=_cparams(("arbitrary",), VMEM_LIMIT),
        name="rglru_bwd" if reverse else "rglru_fwd",
    )(*args)


def _row_validity(rows):
    r4 = ATT_ROWS
    n_blocks = rows // r4
    assert rows % r4 == 0 and rows >= 3 * r4 and rows >= WIN_H

    def for_block(b):
        rq = np.arange(r4)[:, None]
        jr = np.arange(3 * r4)[None, :]
        r = r4 * b + rq
        rk = r4 * b - r4 + jr
        rs = np.clip(r - WIN_H // 2, 0, rows - WIN_H)
        return (rk >= rs) & (rk < rs + WIN_H) & (rk >= 0) & (rk < rows)

    return np.stack([for_block(0), for_block(1), for_block(n_blocks - 1), np.zeros((r4, 3 * r4), bool)])


def _table_kernel(rp_ref, cm_ref, o_ref, *, row_ok):
    w = GRID_W
    n_dr = 2 * WIN_H - 1
    lane = lax.broadcasted_iota(jnp.int32, (w, 2 * w), 1)
    left = lane < w
    cm = cm_ref[...]
    lo, hi = [], []
    for dr in range(n_dr):
        row = jnp.broadcast_to(rp_ref[0, dr:dr + 1, :], (w, 2 * w))
        lo.append(pltpu.roll(row, 0, 1, stride=1, stride_axis=0) + cm)
        hi.append(pltpu.roll(row, w, 1, stride=1, stride_axis=0) + cm)
    neg = jnp.full((w, 2 * w), NEG, F32)
    n_var, r4, n_jr = row_ok.shape
    for v in range(n_var):
        for rq in range(r4):
            for m in range(n_jr // 2):
                dr = 2 * m - rq + (WIN_H - 1) - r4
                ok_a, ok_b = bool(row_ok[v, rq, 2 * m]), bool(row_ok[v, rq, 2 * m + 1])
                if ok_a and ok_b:
                    tile = jnp.where(left, lo[dr], hi[dr + 1])
                elif ok_a:
                    tile = jnp.where(left, lo[dr], neg)
                elif ok_b:
                    tile = jnp.where(left, neg, hi[dr + 1])
                else:
                    tile = neg
                o_ref[v, 0, rq * w:(rq + 1) * w, m * 2 * w:(m + 1) * 2 * w] = tile


def _attn_tables(rpb, rows):
    n_heads = rpb.shape[0]
    w = GRID_W
    row_ok = _row_validity(rows)
    cq = np.arange(w)[:, None]
    ck = np.arange(w)[None, :]
    cs = np.clip(cq - WIN_W // 2, 0, w - WIN_W)
    col_mask = np.where((ck >= cs) & (ck < cs + WIN_W), 0.0, NEG).astype(np.float32)
    col_mask = np.concatenate([col_mask, col_mask], axis=1)
    rp = jnp.concatenate([rpb[..., WIN_W - 1:], jnp.zeros(rpb.shape[:2] + (2 * w - (2 * WIN_W - 1),), F32),
                          rpb[..., :WIN_W - 1]], axis=-1) * LOG2E
    rp = jnp.pad(rp, ((0, 0), (0, 1), (0, 0)))
    return pl.pallas_call(
        functools.partial(_table_kernel, row_ok=row_ok),
        out_shape=jax.ShapeDtypeStruct((row_ok.shape[0], n_heads, ROW_TILE, 3 * ROW_TILE), F32),
        grid=(n_heads,),
        in_specs=[pl.BlockSpec((1, 16, 2 * w), lambda h: (h, 0, 0)),
                  pl.BlockSpec((w, 2 * w), lambda h: (0, 0))],
        out_specs=pl.BlockSpec((row_ok.shape[0], 1, ROW_TILE, 3 * ROW_TILE), lambda h: (0, h, 0, 0)),
        compiler_params=_cparams(("arbitrary",), VMEM_LIMIT),
        name="attn_tables",
    )(rp, jnp.asarray(col_mask))


def _attn_kernel(q_ref, k0_ref, k1_ref, k2_ref, v0_ref, v1_ref, v2_ref, kc_ref, vc_ref, g_ref,
                 t_ref, o_ref, *, heads, scale):
    nt = (((1,), (1,)), ((), ()))
    t = q_ref.shape[0]
    ones_loc = jnp.ones((3 * t, HEAD_DIM), BF16)
    ones_ctx = jnp.ones((kc_ref.shape[0], HEAD_DIM), BF16)
    for h in range(heads):
        sl = slice(h * HEAD_DIM, (h + 1) * HEAD_DIM)
        q = (q_ref[:, sl].astype(F32) * (scale * LOG2E)).astype(BF16)
        k = jnp.concatenate([k0_ref[:, sl], k1_ref[:, sl], k2_ref[:, sl]], axis=0)
        v = jnp.concatenate([v0_ref[:, sl], v1_ref[:, sl], v2_ref[:, sl]], axis=0)
        s_loc = lax.dot_general(q, k, nt, preferred_element_type=F32) + t_ref[0, h]
        s_ctx = lax.dot_general(q, kc_ref[:, sl], nt, preferred_element_type=F32)
        m = jnp.maximum(jnp.max(s_loc, axis=-1, keepdims=True), jnp.max(s_ctx, axis=-1, keepdims=True))
        p_loc = jnp.exp2(s_loc - m).astype(BF16)
        p_ctx = jnp.exp2(s_ctx - m).astype(BF16)
        od = (jnp.dot(p_loc, jnp.concatenate([v, ones_loc], axis=1), preferred_element_type=F32)
              + jnp.dot(p_ctx, jnp.concatenate([vc_ref[:, sl], ones_ctx], axis=1), preferred_element_type=F32))
        o = od[:, :HEAD_DIM] / od[:, HEAD_DIM:]
        o_ref[:, sl] = (o * _silu(g_ref[:, sl].astype(F32))).astype(o_ref.dtype)


def _attention(z, tables, w_att, col0):
    sa = z.shape[0]
    t = ROW_TILE
    nb = sa // t - 1
    hg = ATT_HEADS_PER_STEP
    wl = hg * HEAD_DIM
    n_heads = w_att // HEAD_DIM
    cq, ckk, cv, cg = (c // wl for c in col0)
    clampb = lambda s: jnp.clip(s, 1, nb)
    variant = lambda s: jnp.where(s == 0, 3, jnp.where(s == 1, 0, jnp.where(s == nb, 2, 1)))
    blk = lambda rowf, colb: pl.BlockSpec((t, wl), lambda g, s: (rowf(s), colb + g))
    return pl.pallas_call(
        functools.partial(_attn_kernel, heads=hg, scale=HEAD_DIM ** -0.5),
        out_shape=jax.ShapeDtypeStruct((sa, w_att), BF16),
        grid=(n_heads // hg, nb + 1),
        in_specs=[blk(lambda s: s, cq),
                  blk(lambda s: clampb(s - 1), ckk), blk(lambda s: clampb(s), ckk), blk(lambda s: clampb(s + 1), ckk),
                  blk(lambda s: clampb(s - 1), cv), blk(lambda s: clampb(s), cv), blk(lambda s: clampb(s + 1), cv),
                  blk(lambda s: 0, ckk), blk(lambda s: 0, cv),
                  blk(lambda s: s, cg),
                  pl.BlockSpec((1, hg, t, 3 * t), lambda g, s: (variant(s), g, 0, 0))],
        out_specs=pl.BlockSpec((t, wl), lambda g, s: (s, g)),
        compiler_params=_cparams(("arbitrary", "arbitrary"), VMEM_LIMIT),
        name="nbr_attention",
    )(z, z, z, z, z, z, z, z, z, z, tables)


def _out_proj_kernel(a_ref, b_ref, c_ref, wa_ref, wb_ref, wc_ref, x_ref, g_ref, o_ref, *, tm, n_ctx):
    i = pl.program_id(0)
    acc = (jnp.dot(a_ref[...], wa_ref[0], preferred_element_type=F32)
           + jnp.dot(b_ref[...], wb_ref[0], preferred_element_type=F32)
           + jnp.dot(c_ref[...], wc_ref[0], preferred_element_type=F32))
    rows = i * tm + lax.broadcasted_iota(jnp.int32, acc.shape, 0)
    g = jnp.where(rows < n_ctx, g_ref[1:2, :], g_ref[0:1, :])
    o_ref[...] = x_ref[...] + g * acc


def _out_proj(mix_a, mix_b, mix_c, w_all, l, x_all, mod, n_ctx):
    sa, d = x_all.shape
    wa, wb, wc = mix_a.shape[1], mix_b.shape[1], mix_c.shape[1]
    assert wa == wb and wc % wa == 0
    tm, tn = _row_tile(sa, 1280), 512
    gate_blk0 = 2 * d // tn
    return pl.pallas_call(
        functools.partial(_out_proj_kernel, tm=tm, n_ctx=n_ctx),
        out_shape=jax.ShapeDtypeStruct((sa, d), F32),
        grid=(sa // tm, d // tn),
        in_specs=[pl.BlockSpec((tm, wa), lambda i, j: (i, 0)),
                  pl.BlockSpec((tm, wb), lambda i, j: (i, 0)),
                  pl.BlockSpec((tm, wc), lambda i, j: (i, 0)),
                  pl.BlockSpec((1, wa, tn), lambda i, j: (l, 0, j)),
                  pl.BlockSpec((1, wb, tn), lambda i, j: (l, 1, j)),
                  pl.BlockSpec((1, wc, tn), lambda i, j: (l, (wa + wb) // wc, j)),
                  pl.BlockSpec((tm, tn), lambda i, j: (i, j)),
                  pl.BlockSpec((8, tn), lambda i, j: (0, gate_blk0 + j))],
        out_specs=pl.BlockSpec((tm, tn), lambda i, j: (i, j)),
        compiler_params=_cparams(("arbitrary", "arbitrary"), VMEM_LIMIT),
        name="out_proj",
    )(mix_a, mix_b, mix_c, w_all, w_all, w_all, x_all, mod)


def kernel(x, c, ctx, c_ctx, w_ada, b_ada, norm_g, w_in, conv_w, conv_b, ln_g, ln_b, w_pw, b_pw,
           rconv_w, rconv_b, w_r, b_r, w_i, b_i, lam, rpb, w_out, final_g):
    bsz, s, d = x.shape
    n_ctx = ctx.shape[1]
    depth = w_ada.shape[0]
    w_conv = conv_w.shape[2]
    w_rec = rconv_w.shape[3]
    w_att = rpb.shape[1] * HEAD_DIM
    assert bsz == 1 and n_ctx == ROW_TILE and s % ROW_TILE == 0 and ATT_ROWS * GRID_W == ROW_TILE
    assert w_conv == w_rec == 1024
    rows = s // GRID_W
    col_q = 3 * w_conv + 2 * w_rec
    col0 = (col_q, col_q + w_att, col_q + 2 * w_att, col_q + 3 * w_att)

    cc = jnp.zeros((8, d), F32).at[0].set(c[0]).at[1].set(c_ctx)
    mods = _ada_all_layers(cc, w_ada, b_ada)
    x_all = jnp.concatenate([ctx[0], x[0]], axis=0)
    w_in_b = w_in.astype(BF16)
    w_out_b = w_out.astype(BF16)

    for l in range(depth):
        mod = mods[l]
        h = _norm_mod(x_all, norm_g[l], mod.reshape(8, 1, 3 * d))
        z = _in_proj(h, w_in_b, l)
        mix_a = _conv_mixer(z, conv_w[l], conv_b[l], ln_g[l], ln_b[l], w_pw[l].astype(BF16), b_pw[l])
        hf = _rglru(z, None, rconv_w[l, 0], rconv_b[l, 0], w_r[l, 0], w_i[l, 0], b_r[l, 0], b_i[l, 0],
                    lam[l, 0], reverse=False)
        mix_b = _rglru(z, hf, rconv_w[l, 1], rconv_b[l, 1], w_r[l, 1], w_i[l, 1], b_r[l, 1], b_i[l, 1],
                       lam[l, 1], reverse=True)
        mix_c = _attention(z, _attn_tables(rpb[l], rows), w_att, col0)
        x_all = _out_proj(mix_a, mix_b, mix_c, w_out_b, l, x_all, mod, n_ctx)

    return _final_norm(x_all, final_g, n_ctx // ROW_TILE)[None]
```

```python
import functools

import numpy as np
import jax
import jax.numpy as jnp
from jax import lax
from jax.experimental import pallas as pl
from jax.experimental.pallas import tpu as pltpu

F32 = jnp.float32
BF16 = jnp.bfloat16

GRID_W = 64
WIN_H = 8
WIN_W = 16
CONV_K = 31
REC_CONV_K = 4
REC_HEADS = 8
REC_C = 8.0
HEAD_DIM = 128
EPS = 1e-6
NEG = -1e30
LOG2E = 1.4426950408889634
LANES = 128

ROW_TILE = 256
HALO = 16
ATT_ROWS = 4
ATT_HEADS_PER_STEP = 16
VMEM_LIMIT = 56 * 1024 * 1024


def _cparams(sem, vmem=None):
    return pltpu.CompilerParams(dimension_semantics=sem, vmem_limit_bytes=vmem)


def _sigmoid(x):
    return 1.0 / (1.0 + jnp.exp2(x * -LOG2E))


def _silu(x):
    return x * _sigmoid(x)


def _ada_kernel(cc_ref, w_ref, b_ref, o_ref):
    a = _silu(cc_ref[...]).astype(BF16)
    w = w_ref[0].astype(BF16)
    o_ref[0] = jnp.dot(a, w, preferred_element_type=F32) + b_ref[0]


def _ada_all_layers(cc, w_ada, b_ada):
    depth, d, n = w_ada.shape
    tn = 512
    return pl.pallas_call(
        _ada_kernel,
        out_shape=jax.ShapeDtypeStruct((depth, 8, n), F32),
        grid=(depth, n // tn),
        in_specs=[pl.BlockSpec((8, d), lambda l, j: (0, 0)),
                  pl.BlockSpec((1, d, tn), lambda l, j: (l, 0, j)),
                  pl.BlockSpec((1, 1, tn), lambda l, j: (l, 0, j))],
        out_specs=pl.BlockSpec((1, 8, tn), lambda l, j: (l, 0, j)),
        compiler_params=_cparams(("arbitrary", "arbitrary"), VMEM_LIMIT),
        name="ada_mod",
    )(cc, w_ada, b_ada.reshape(depth, 1, n))


def _norm_mod_kernel(x_ref, g_ref, mod_ref, o_ref, *, d):
    x = x_ref[...]
    y = x * lax.rsqrt(jnp.mean(x * x, axis=-1, keepdims=True) + EPS) * g_ref[...]
    mod = mod_ref[0]
    o_ref[...] = (y * (1.0 + mod[:, d:2 * d]) + mod[:, :d]).astype(o_ref.dtype)


def _norm_mod(x_all, g, mod3):
    sa, d = x_all.shape
    return pl.pallas_call(
        functools.partial(_norm_mod_kernel, d=d),
        out_shape=jax.ShapeDtypeStruct((sa, d), BF16),
        grid=(sa // ROW_TILE,),
        in_specs=[pl.BlockSpec((ROW_TILE, d), lambda i: (i, 0)),
                  pl.BlockSpec((1, d), lambda i: (0, 0)),
                  pl.BlockSpec((1, 1, 3 * d), lambda i: (jnp.where(i == 0, 1, 0), 0, 0))],
        out_specs=pl.BlockSpec((ROW_TILE, d), lambda i: (i, 0)),
        compiler_params=_cparams(("arbitrary",), VMEM_LIMIT),
        name="norm_mod",
    )(x_all, g.reshape(1, d), mod3)


def _norm_mod_first_kernel(ctx_ref, x_ref, g_ref, mod_ref, h_ref, xall_ref, *, d):
    def emit(src_ref):
        x = src_ref[...]
        xall_ref[...] = x
        y = x * lax.rsqrt(jnp.mean(x * x, axis=-1, keepdims=True) + EPS) * g_ref[...]
        mod = mod_ref[0]
        h_ref[...] = (y * (1.0 + mod[:, d:2 * d]) + mod[:, :d]).astype(h_ref.dtype)

    pl.when(pl.program_id(0) == 0)(lambda: emit(ctx_ref))
    pl.when(pl.program_id(0) > 0)(lambda: emit(x_ref))


def _norm_mod_first(ctx2, x2, g, mod3):
    n_ctx, d = ctx2.shape
    sa = n_ctx + x2.shape[0]
    assert n_ctx == ROW_TILE
    return pl.pallas_call(
        functools.partial(_norm_mod_first_kernel, d=d),
        out_shape=(jax.ShapeDtypeStruct((sa, d), BF16), jax.ShapeDtypeStruct((sa, d), F32)),
        grid=(sa // ROW_TILE,),
        in_specs=[pl.BlockSpec((ROW_TILE, d), lambda i: (0, 0)),
                  pl.BlockSpec((ROW_TILE, d), lambda i: (jnp.maximum(i - 1, 0), 0)),
                  pl.BlockSpec((1, d), lambda i: (0, 0)),
                  pl.BlockSpec((1, 1, 3 * d), lambda i: (jnp.where(i == 0, 1, 0), 0, 0))],
        out_specs=(pl.BlockSpec((ROW_TILE, d), lambda i: (i, 0)),
                   pl.BlockSpec((ROW_TILE, d), lambda i: (i, 0))),
        compiler_params=_cparams(("arbitrary",), VMEM_LIMIT),
        name="norm_mod_first",
    )(ctx2, x2, g.reshape(1, d), mod3)


def _final_norm_kernel(x_ref, g_ref, o_ref):
    x = x_ref[...]
    o_ref[...] = x * lax.rsqrt(jnp.mean(x * x, axis=-1, keepdims=True) + EPS) * g_ref[...]


def _final_norm(x_all, g, n_ctx_tiles):
    sa, d = x_all.shape
    s = sa - n_ctx_tiles * ROW_TILE
    return pl.pallas_call(
        _final_norm_kernel,
        out_shape=jax.ShapeDtypeStruct((s, d), F32),
        grid=(s // ROW_TILE,),
        in_specs=[pl.BlockSpec((ROW_TILE, d), lambda i: (i + n_ctx_tiles, 0)),
                  pl.BlockSpec((1, d), lambda i: (0, 0))],
        out_specs=pl.BlockSpec((ROW_TILE, d), lambda i: (i, 0)),
        compiler_params=_cparams(("arbitrary",), VMEM_LIMIT),
        name="final_norm",
    )(x_all, g.reshape(1, d))


def _matmul_kernel(a_ref, b_ref, o_ref):
    o_ref[...] = jnp.dot(a_ref[...], b_ref[0].astype(BF16), preferred_element_type=F32).astype(o_ref.dtype)


def _row_tile(sa, cap):
    best = ROW_TILE
    for t in range(ROW_TILE, cap + 1, ROW_TILE):
        if sa % t == 0:
            best = t
    return best


def _in_proj(h, w_all, l, first_col):
    sa, d = h.shape
    n = w_all.shape[2]
    tm, tn = _row_tile(sa, 1280), 512
    assert first_col % tn == 0
    rot, n_col_blocks = first_col // tn, n // tn
    return pl.pallas_call(
        _matmul_kernel,
        out_shape=jax.ShapeDtypeStruct((sa, n), BF16),
        grid=(sa // tm, n // tn),
        in_specs=[pl.BlockSpec((tm, d), lambda i, j: (i, 0)),
                  pl.BlockSpec((1, d, tn), lambda i, j: (l, 0, (j + rot) % n_col_blocks))],
        out_specs=pl.BlockSpec((tm, tn), lambda i, j: (i, j)),
        compiler_params=_cparams(("arbitrary", "arbitrary"), VMEM_LIMIT),
        name="in_proj",
    )(h, w_all)


def _conv_kernel(val_ref, glu_ref, gate_ref, vb_ref, gb_ref, va_ref, ga_ref,
                 cw_ref, cb_ref, lng_ref, lnb_ref, wpw_ref, bpw_ref, o_ref, u_scr, *, t):
    i = pl.program_id(0)
    n = pl.num_programs(0)

    def glu(v_ref, g_ref):
        return v_ref[...].astype(F32) * _sigmoid(g_ref[...].astype(F32))

    has_before = jnp.where(i >= 2, 1.0, 0.0)
    has_after = jnp.where((i >= 1) & (i < n - 1), 1.0, 0.0)
    u_scr[0:HALO, :] = glu(vb_ref, gb_ref) * has_before
    u_scr[HALO:HALO + t, :] = glu(val_ref, glu_ref)
    u_scr[HALO + t:HALO + t + HALO, :] = glu(va_ref, ga_ref) * has_after

    first = HALO - CONV_K // 2
    blocks = []
    for lb in range(u_scr.shape[1] // LANES):
        lanes = slice(lb * LANES, (lb + 1) * LANES)
        acc = None
        for res in range(8):
            g = None
            for off in range(res, first + CONV_K, 8):
                if off < first:
                    continue
                term = cw_ref[off - first:off - first + 1, lanes] * u_scr[pl.ds(off - res, t + 8), lanes]
                g = term if g is None else g + term
            if res:
                g = pltpu.roll(g, t + 8 - res, 0)
            acc = g[:t] if acc is None else acc + g[:t]
        blocks.append(acc + cb_ref[:, lanes])
    acc = jnp.concatenate(blocks, axis=1)

    mu = jnp.mean(acc, axis=-1, keepdims=True)
    cen = acc - mu
    var = jnp.mean(cen * cen, axis=-1, keepdims=True)
    y = _silu(cen * lax.rsqrt(var + EPS) * lng_ref[...] + lnb_ref[...])
    out = jnp.dot(y.astype(BF16), wpw_ref[...], preferred_element_type=F32) + bpw_ref[...]
    o_ref[...] = (out * _silu(gate_ref[...].astype(F32))).astype(o_ref.dtype)


def _conv_mixer(z, cb0, cw, cb, lng, lnb, wpw, bpw):
    sa = z.shape[0]
    c = cw.shape[1]
    t = ROW_TILE
    hb = t // HALO
    last_h = sa // HALO - 1
    cw_p = jnp.zeros((32, c), F32).at[:CONV_K].set(cw)
    row = lambda v: v.reshape(1, c)
    before = lambda col: pl.BlockSpec((HALO, c), lambda i: (jnp.maximum(i * hb - 1, 0), col))
    after = lambda col: pl.BlockSpec((HALO, c), lambda i: (jnp.minimum((i + 1) * hb, last_h), col))
    full = lambda shape: pl.BlockSpec(shape, lambda i: (0, 0))
    return pl.pallas_call(
        functools.partial(_conv_kernel, t=t),
        out_shape=jax.ShapeDtypeStruct((sa, c), BF16),
        grid=(sa // t,),
        in_specs=[pl.BlockSpec((t, c), lambda i: (i, cb0)),
                  pl.BlockSpec((t, c), lambda i: (i, cb0 + 1)),
                  pl.BlockSpec((t, c), lambda i: (i, cb0 + 2)),
                  before(cb0), before(cb0 + 1), after(cb0), after(cb0 + 1),
                  full((32, c)), full((1, c)), full((1, c)), full((1, c)),
                  full((c, c)), full((1, c))],
        out_specs=pl.BlockSpec((t, c), lambda i: (i, 0)),
        scratch_shapes=[pltpu.VMEM((t + 2 * HALO, c), F32)],
        compiler_params=_cparams(("arbitrary",), VMEM_LIMIT),
        name="conv_mixer",
    )(z, z, z, z, z, z, z, cw_p, row(cb), row(lng), row(lnb), wpw, row(bpw))


SUB = 8


def _scan_tile(a, b, h_in, reverse):
    t, c = a.shape
    groups = t // SUB
    a = a.reshape(groups, SUB, c)
    b = b.reshape(groups, SUB, c)
    rows = lax.broadcasted_iota(jnp.int32, a.shape, 1)
    s = 1
    while s < SUB:
        if reverse:
            keep = rows < SUB - s
            shift = SUB - s
        else:
            keep = rows >= s
            shift = s
        a_sh = jnp.where(keep, pltpu.roll(a, shift, 1), 1.0)
        b_sh = jnp.where(keep, pltpu.roll(b, shift, 1), 0.0)
        b = a * b_sh + b
        a = a * a_sh
        s *= 2
    out = [None] * groups
    h = h_in
    for g in (range(groups - 1, -1, -1) if reverse else range(groups)):
        hg = a[g] * h + b[g]
        out[g] = hg
        h = hg[0:1, :] if reverse else hg[SUB - 1:SUB, :]
    return jnp.concatenate(out, axis=0), h


def _rglru_kernel(*refs, t, reverse):
    if reverse:
        (x_ref, xh_ref, hf_ref, gate_ref, cw_ref, cb_ref, wri_ref, br_ref, bi_ref, lam_ref,
         o_ref, carry_scr, xs_scr) = refs
    else:
        (x_ref, xh_ref, cw_ref, cb_ref, wri_ref, br_ref, bi_ref, lam_ref,
         o_ref, carry_scr, xs_scr) = refs
    s = pl.program_id(0)

    @pl.when(s == 0)
    def _():
        carry_scr[...] = jnp.zeros_like(carry_scr)

    c = x_ref.shape[1]
    has_halo = jnp.where(s >= 2, 1.0, 0.0)
    x = x_ref[...].astype(F32)
    halo = xh_ref[...].astype(F32) * has_halo
    if reverse:
        xs_scr[0:t, :] = x
        xs_scr[t:t + HALO, :] = halo
        base = 0
    else:
        xs_scr[0:HALO, :] = halo
        xs_scr[HALO:HALO + t, :] = x
        base = HALO - (REC_CONV_K - 1)
    xc = jnp.zeros((t, c), F32) + cb_ref[...]
    for k in range(REC_CONV_K):
        xc = xc + cw_ref[k:k + 1, :] * xs_scr[pl.ds(base + k, t), :]

    xcb = xc.astype(BF16)
    blk = c // REC_HEADS
    r_parts, i_parts = [], []
    for hh in range(REC_HEADS):
        ri = jnp.dot(xcb[:, hh * blk:(hh + 1) * blk], wri_ref[hh], preferred_element_type=F32)
        r_parts.append(ri[:, :blk])
        i_parts.append(ri[:, blk:])
    r = _sigmoid(jnp.concatenate(r_parts, axis=1) + br_ref[...])
    ig = _sigmoid(jnp.concatenate(i_parts, axis=1) + bi_ref[...])

    lam = lam_ref[...]
    softplus_neg_lam = jnp.maximum(-lam, 0.0) + jnp.log(1.0 + jnp.exp(-jnp.abs(lam)))
    a = jnp.exp2(r * (-REC_C * LOG2E * softplus_neg_lam))
    one_m_a2 = 1.0 - a * a
    b = (one_m_a2 * lax.rsqrt(jnp.maximum(one_m_a2, 1e-30))) * (ig * xc)

    h, h_out = _scan_tile(a, b, carry_scr[0:1, :], reverse)
    carry_scr[0:1, :] = h_out

    if reverse:
        y = hf_ref[...].astype(F32) + h
        o_ref[...] = (y * _silu(gate_ref[...].astype(F32))).astype(o_ref.dtype)
    else:
        o_ref[...] = h.astype(o_ref.dtype)


def _rglru(z, cb0, hf, cw, cb, w_r, w_i, b_r, b_i, lam, reverse):
    sa = z.shape[0]
    c = cw.shape[1]
    t = ROW_TILE
    n = sa // t
    hb = t // HALO
    last_h = sa // HALO - 1
    wri = jnp.concatenate([w_r, w_i], axis=-1).astype(BF16)
    cw_p = jnp.zeros((8, c), F32).at[:REC_CONV_K].set(cw)
    row = lambda v: v.reshape(1, c)
    full = lambda shape: pl.BlockSpec(shape, lambda s: (0,) * len(shape))
    if reverse:
        tile = lambda s: jnp.where(s == 0, 0, n - s)
        halo_spec = pl.BlockSpec((HALO, c), lambda s: (jnp.minimum((tile(s) + 1) * hb, last_h), cb0))
    else:
        tile = lambda s: s
        halo_spec = pl.BlockSpec((HALO, c), lambda s: (jnp.maximum(tile(s) * hb - 1, 0), cb0))
    in_specs = [pl.BlockSpec((t, c), lambda s: (tile(s), cb0)), halo_spec]
    args = [z, z]
    if reverse:
        in_specs += [pl.BlockSpec((t, c), lambda s: (tile(s), 0)),
                     pl.BlockSpec((t, c), lambda s: (tile(s), cb0 + 1))]
        args += [hf, z]
    in_specs += [full((8, c)), full((1, c)), full(wri.shape), full((1, c)), full((1, c)), full((1, c))]
    args += [cw_p, row(cb), wri, row(b_r), row(b_i), row(lam)]
    return pl.pallas_call(
        functools.partial(_rglru_kernel, t=t, reverse=reverse),
        out_shape=jax.ShapeDtypeStruct((sa, c), BF16),
        grid=(n,),
        in_specs=in_specs,
        out_specs=pl.BlockSpec((t, c), lambda s: (tile(s), 0)),
        scratch_shapes=[pltpu.VMEM((8, c), F32), pltpu.VMEM((t + HALO, c), F32)],
        compiler_params=_cparams(("arbitrary",), VMEM_LIMIT),
        name="rglru_bwd" if reverse else "rglru_fwd",
    )(*args)


def _row_validity(rows):
    r4 = ATT_ROWS
    n_blocks = rows // r4
    assert rows % r4 == 0 and rows >= 3 * r4 and rows >= WIN_H

    def for_block(b):
        rq = np.arange(r4)[:, None]
        jr = np.arange(3 * r4)[None, :]
        r = r4 * b + rq
        rk = r4 * b - r4 + jr
        rs = np.clip(r - WIN_H // 2, 0, rows - WIN_H)
        return (rk >= rs) & (rk < rs + WIN_H) & (rk >= 0) & (rk < rows)

    return np.stack([for_block(0), for_block(1), for_block(n_blocks - 1), np.zeros((r4, 3 * r4), bool)])


def _table_kernel(rp_ref, cm_ref, o_ref, *, row_ok):
    w = GRID_W
    n_dr = 2 * WIN_H - 1
    lane = lax.broadcasted_iota(jnp.int32, (w, 2 * w), 1)
    left = lane < w
    cm = cm_ref[...]
    lo, hi = [], []
    for dr in range(n_dr):
        row = jnp.broadcast_to(rp_ref[0, dr:dr + 1, :], (w, 2 * w))
        lo.append(pltpu.roll(row, 0, 1, stride=1, stride_axis=0) + cm)
        hi.append(pltpu.roll(row, w, 1, stride=1, stride_axis=0) + cm)
    neg = jnp.full((w, 2 * w), NEG, F32)
    n_var, r4, n_jr = row_ok.shape
    for v in range(n_var):
        for rq in range(r4):
            for m in range(n_jr // 2):
                dr = 2 * m - rq + (WIN_H - 1) - r4
                ok_a, ok_b = bool(row_ok[v, rq, 2 * m]), bool(row_ok[v, rq, 2 * m + 1])
                if ok_a and ok_b:
                    tile = jnp.where(left, lo[dr], hi[dr + 1])
                elif ok_a:
                    tile = jnp.where(left, lo[dr], neg)
                elif ok_b:
                    tile = jnp.where(left, neg, hi[dr + 1])
                else:
                    tile = neg
                o_ref[v, 0, rq * w:(rq + 1) * w, m * 2 * w:(m + 1) * 2 * w] = tile


def _attn_tables(rpb, rows):
    n_heads = rpb.shape[0]
    w = GRID_W
    row_ok = _row_validity(rows)
    cq = np.arange(w)[:, None]
    ck = np.arange(w)[None, :]
    cs = np.clip(cq - WIN_W // 2, 0, w - WIN_W)
    col_mask = np.where((ck >= cs) & (ck < cs + WIN_W), 0.0, NEG).astype(np.float32)
    col_mask = np.concatenate([col_mask, col_mask], axis=1)
    rp = jnp.concatenate([rpb[..., WIN_W - 1:], jnp.zeros(rpb.shape[:2] + (2 * w - (2 * WIN_W - 1),), F32),
                          rpb[..., :WIN_W - 1]], axis=-1) * LOG2E
    rp = jnp.pad(rp, ((0, 0), (0, 1), (0, 0)))
    return pl.pallas_call(
        functools.partial(_table_kernel, row_ok=row_ok),
        out_shape=jax.ShapeDtypeStruct((row_ok.shape[0], n_heads, ROW_TILE, 3 * ROW_TILE), F32),
        grid=(n_heads,),
        in_specs=[pl.BlockSpec((1, 16, 2 * w), lambda h: (h, 0, 0)),
                  pl.BlockSpec((w, 2 * w), lambda h: (0, 0))],
        out_specs=pl.BlockSpec((row_ok.shape[0], 1, ROW_TILE, 3 * ROW_TILE), lambda h: (0, h, 0, 0)),
        compiler_params=_cparams(("arbitrary",), VMEM_LIMIT),
        name="attn_tables",
    )(rp, jnp.asarray(col_mask))


def _attn_kernel(q_ref, k0_ref, k1_ref, k2_ref, v0_ref, v1_ref, v2_ref, kc_ref, vc_ref, g_ref,
                 t_ref, o_ref, *, heads, scale):
    nt = (((1,), (1,)), ((), ()))
    t = q_ref.shape[0]
    ones_loc = jnp.ones((3 * t, HEAD_DIM), BF16)
    ones_ctx = jnp.ones((kc_ref.shape[0], HEAD_DIM), BF16)

    def scores(h):
        sl = slice(h * HEAD_DIM, (h + 1) * HEAD_DIM)
        q = (q_ref[:, sl].astype(F32) * (scale * LOG2E)).astype(BF16)
        k = jnp.concatenate([k0_ref[:, sl], k1_ref[:, sl], k2_ref[:, sl]], axis=0)
        s_loc = lax.dot_general(q, k, nt, preferred_element_type=F32) + t_ref[0, h]
        s_ctx = lax.dot_general(q, kc_ref[:, sl], nt, preferred_element_type=F32)
        return s_loc, s_ctx

    nxt = scores(0)
    for h in range(heads):
        sl = slice(h * HEAD_DIM, (h + 1) * HEAD_DIM)
        s_loc, s_ctx = nxt
        if h + 1 < heads:
            nxt = scores(h + 1)
        v = jnp.concatenate([v0_ref[:, sl], v1_ref[:, sl], v2_ref[:, sl]], axis=0)
        m = jnp.maximum(jnp.max(s_loc, axis=-1, keepdims=True), jnp.max(s_ctx, axis=-1, keepdims=True))
        p_loc = jnp.exp2(s_loc - m).astype(BF16)
        p_ctx = jnp.exp2(s_ctx - m).astype(BF16)
        od = (jnp.dot(p_loc, jnp.concatenate([v, ones_loc], axis=1), preferred_element_type=F32)
              + jnp.dot(p_ctx, jnp.concatenate([vc_ref[:, sl], ones_ctx], axis=1), preferred_element_type=F32))
        o = od[:, :HEAD_DIM] / od[:, HEAD_DIM:]
        o_ref[:, sl] = (o * _silu(g_ref[:, sl].astype(F32))).astype(o_ref.dtype)


def _attention(z, tables, w_att, col0):
    sa = z.shape[0]
    t = ROW_TILE
    nb = sa // t - 1
    hg = ATT_HEADS_PER_STEP
    wl = hg * HEAD_DIM
    n_heads = w_att // HEAD_DIM
    assert all(c % wl == 0 for c in col0)
    cq, ckk, cv, cg = (c // wl for c in col0)
    clampb = lambda s: jnp.clip(s, 1, nb)
    variant = lambda s: jnp.where(s == 0, 3, jnp.where(s == 1, 0, jnp.where(s == nb, 2, 1)))
    blk = lambda rowf, colb: pl.BlockSpec((t, wl), lambda g, s: (rowf(s), colb + g))
    return pl.pallas_call(
        functools.partial(_attn_kernel, heads=hg, scale=HEAD_DIM ** -0.5),
        out_shape=jax.ShapeDtypeStruct((sa, w_att), BF16),
        grid=(n_heads // hg, nb + 1),
        in_specs=[blk(lambda s: s, cq),
                  blk(lambda s: clampb(s - 1), ckk), blk(lambda s: clampb(s), ckk), blk(lambda s: clampb(s + 1), ckk),
                  blk(lambda s: clampb(s - 1), cv), blk(lambda s: clampb(s), cv), blk(lambda s: clampb(s + 1), cv),
                  blk(lambda s: 0, ckk), blk(lambda s: 0, cv),
                  blk(lambda s: s, cg),
                  pl.BlockSpec((1, hg, t, 3 * t), lambda g, s: (variant(s), g, 0, 0))],
        out_specs=pl.BlockSpec((t, wl), lambda g, s: (s, g)),
        compiler_params=_cparams(("arbitrary", "arbitrary"), VMEM_LIMIT),
        name="nbr_attention",
    )(z, z, z, z, z, z, z, z, z, z, tables)


def _out_proj_kernel(a_ref, b_ref, c_ref, wa_ref, wb_ref, wc_ref, x_ref, g_ref, o_ref, *, tm, n_ctx):
    i = pl.program_id(0)
    acc = (jnp.dot(a_ref[...], wa_ref[0].astype(BF16), preferred_element_type=F32)
           + jnp.dot(b_ref[...], wb_ref[0].astype(BF16), preferred_element_type=F32)
           + jnp.dot(c_ref[...], wc_ref[0].astype(BF16), preferred_element_type=F32))
    rows = i * tm + lax.broadcasted_iota(jnp.int32, acc.shape, 0)
    g = jnp.where(rows < n_ctx, g_ref[1:2, :], g_ref[0:1, :])
    o_ref[...] = x_ref[...] + g * acc


def _out_proj(mix_a, mix_b, mix_c, w_all, l, x_all, mod, n_ctx):
    sa, d = x_all.shape
    wa, wb, wc = mix_a.shape[1], mix_b.shape[1], mix_c.shape[1]
    assert wa == wb and wc % wa == 0
    tm, tn = _row_tile(sa, 1280), 512
    gate_blk0 = 2 * d // tn
    return pl.pallas_call(
        functools.partial(_out_proj_kernel, tm=tm, n_ctx=n_ctx),
        out_shape=jax.ShapeDtypeStruct((sa, d), F32),
        grid=(sa // tm, d // tn),
        in_specs=[pl.BlockSpec((tm, wa), lambda i, j: (i, 0)),
                  pl.BlockSpec((tm, wb), lambda i, j: (i, 0)),
                  pl.BlockSpec((tm, wc), lambda i, j: (i, 0)),
                  pl.BlockSpec((1, wa, tn), lambda i, j: (l, 0, j)),
                  pl.BlockSpec((1, wb, tn), lambda i, j: (l, 1, j)),
                  pl.BlockSpec((1, wc, tn), lambda i, j: (l, (wa + wb) // wc, j)),
                  pl.BlockSpec((tm, tn), lambda i, j: (i, j)),
                  pl.BlockSpec((8, tn), lambda i, j: (0, gate_blk0 + j))],
        out_specs=pl.BlockSpec((tm, tn), lambda i, j: (i, j)),
        compiler_params=_cparams(("arbitrary", "arbitrary"), VMEM_LIMIT),
        name="out_proj",
    )(mix_a, mix_b, mix_c, w_all, w_all, w_all, x_all, mod)


def kernel(x, c, ctx, c_ctx, w_ada, b_ada, norm_g, w_in, conv_w, conv_b, ln_g, ln_b, w_pw, b_pw,
           rconv_w, rconv_b, w_r, b_r, w_i, b_i, lam, rpb, w_out, final_g):
    bsz, s, d = x.shape
    n_ctx = ctx.shape[1]
    depth = w_ada.shape[0]
    w_conv = conv_w.shape[2]
    w_rec = rconv_w.shape[3]
    w_att = rpb.shape[1] * HEAD_DIM
    assert bsz == 1 and n_ctx == ROW_TILE and s % ROW_TILE == 0 and ATT_ROWS * GRID_W == ROW_TILE
    assert w_conv == w_rec == 1024
    rows = s // GRID_W
    first_col = 3 * w_conv + 2 * w_rec
    col0 = (0, w_att, 2 * w_att, 3 * w_att)
    conv_cb0 = 4 * w_att // w_conv
    rec_cb0 = conv_cb0 + 3

    cc = jnp.zeros((8, d), F32).at[0].set(c[0]).at[1].set(c_ctx)
    mods = _ada_all_layers(cc, w_ada, b_ada)

    for l in range(depth):
        mod = mods[l]
        if l == 0:
            h, x_all = _norm_mod_first(ctx[0], x[0], norm_g[l], mod.reshape(8, 1, 3 * d))
        else:
            h = _norm_mod(x_all, norm_g[l], mod.reshape(8, 1, 3 * d))
        z = _in_proj(h, w_in, l, first_col)
        mix_a = _conv_mixer(z, conv_cb0, conv_w[l], conv_b[l], ln_g[l], ln_b[l], w_pw[l].astype(BF16), b_pw[l])
        hf = _rglru(z, rec_cb0, None, rconv_w[l, 0], rconv_b[l, 0], w_r[l, 0], w_i[l, 0], b_r[l, 0],
                    b_i[l, 0], lam[l, 0], reverse=False)
        mix_b = _rglru(z, rec_cb0, hf, rconv_w[l, 1], rconv_b[l, 1], w_r[l, 1], w_i[l, 1], b_r[l, 1],
                       b_i[l, 1], lam[l, 1], reverse=True)
        mix_c = _attention(z, _attn_tables(rpb[l], rows), w_att, col0)
        x_all = _out_proj(mix_a, mix_b, mix_c, w_out, l, x_all, mod, n_ctx)

    return _final_norm(x_all, final_g, n_ctx // ROW_TILE)[None]
```

```python
import functools

import numpy as np
import jax
import jax.numpy as jnp
from jax import lax
from jax.experimental import pallas as pl
from jax.experimental.pallas import tpu as pltpu

F32 = jnp.float32
BF16 = jnp.bfloat16

GRID_W = 64
WIN_H = 8
WIN_W = 16
CONV_K = 31
REC_CONV_K = 4
REC_HEADS = 8
REC_C = 8.0
HEAD_DIM = 128
EPS = 1e-6
NEG = -1e30
LOG2E = 1.4426950408889634
LANES = 128

ROW_TILE = 256
HALO = 16
ATT_ROWS = 4
ATT_HEADS_PER_STEP = 16
VMEM_LIMIT = 56 * 1024 * 1024


def _cparams(sem, vmem=None):
    return pltpu.CompilerParams(dimension_semantics=sem, vmem_limit_bytes=vmem)


def _sigmoid(x):
    return 1.0 / (1.0 + jnp.exp2(x * -LOG2E))


def _silu(x):
    return x * _sigmoid(x)


def _ada_kernel(cc_ref, w_ref, b_ref, o_ref):
    a = _silu(cc_ref[...]).astype(BF16)
    w = w_ref[0].astype(BF16)
    o_ref[0] = jnp.dot(a, w, preferred_element_type=F32) + b_ref[0]


def _ada_all_layers(cc, w_ada, b_ada):
    depth, d, n = w_ada.shape
    tn = 512
    return pl.pallas_call(
        _ada_kernel,
        out_shape=jax.ShapeDtypeStruct((depth, 8, n), F32),
        grid=(depth, n // tn),
        in_specs=[pl.BlockSpec((8, d), lambda l, j: (0, 0)),
                  pl.BlockSpec((1, d, tn), lambda l, j: (l, 0, j)),
                  pl.BlockSpec((1, 1, tn), lambda l, j: (l, 0, j))],
        out_specs=pl.BlockSpec((1, 8, tn), lambda l, j: (l, 0, j)),
        compiler_params=_cparams(("arbitrary", "arbitrary"), VMEM_LIMIT),
        name="ada_mod",
    )(cc, w_ada, b_ada.reshape(depth, 1, n))


NORM_ROWS = 16


def _norm_mod_rows(src_ref, g_ref, mod_ref, h_ref, sc_scr, sh_scr, copy_ref, *, d):
    mod = mod_ref[0]
    sc_scr[...] = g_ref[...] * (1.0 + mod[:, d:2 * d])
    sh_scr[...] = mod[:, :d]

    def trip(r, carry):
        rows = pl.ds(pl.multiple_of(r * NORM_ROWS, NORM_ROWS), NORM_ROWS)
        x = src_ref[rows, :]
        if copy_ref is not None:
            copy_ref[rows, :] = x
        rs = lax.rsqrt(jnp.mean(x * x, axis=-1, keepdims=True) + EPS)
        h_ref[rows, :] = (src_ref[rows, :] * rs * sc_scr[...] + sh_scr[...]).astype(h_ref.dtype)
        return carry

    lax.fori_loop(0, src_ref.shape[0] // NORM_ROWS, trip, 0, unroll=4)


def _norm_mod_kernel(x_ref, g_ref, mod_ref, o_ref, sc_scr, sh_scr, *, d):
    _norm_mod_rows(x_ref, g_ref, mod_ref, o_ref, sc_scr, sh_scr, None, d=d)


def _norm_mod(x_all, g, mod3):
    sa, d = x_all.shape
    return pl.pallas_call(
        functools.partial(_norm_mod_kernel, d=d),
        out_shape=jax.ShapeDtypeStruct((sa, d), BF16),
        grid=(sa // ROW_TILE,),
        in_specs=[pl.BlockSpec((ROW_TILE, d), lambda i: (i, 0)),
                  pl.BlockSpec((1, d), lambda i: (0, 0)),
                  pl.BlockSpec((1, 1, 3 * d), lambda i: (jnp.where(i == 0, 1, 0), 0, 0))],
        out_specs=pl.BlockSpec((ROW_TILE, d), lambda i: (i, 0)),
        scratch_shapes=[pltpu.VMEM((1, d), F32), pltpu.VMEM((1, d), F32)],
        compiler_params=_cparams(("arbitrary",), VMEM_LIMIT),
        name="norm_mod",
    )(x_all, g.reshape(1, d), mod3)


def _norm_mod_first_kernel(ctx_ref, x_ref, g_ref, mod_ref, h_ref, xall_ref, sc_scr, sh_scr, *, d):
    emit = lambda src_ref: _norm_mod_rows(src_ref, g_ref, mod_ref, h_ref, sc_scr, sh_scr, xall_ref, d=d)
    pl.when(pl.program_id(0) == 0)(lambda: emit(ctx_ref))
    pl.when(pl.program_id(0) > 0)(lambda: emit(x_ref))


def _norm_mod_first(ctx2, x2, g, mod3):
    n_ctx, d = ctx2.shape
    sa = n_ctx + x2.shape[0]
    assert n_ctx == ROW_TILE
    return pl.pallas_call(
        functools.partial(_norm_mod_first_kernel, d=d),
        out_shape=(jax.ShapeDtypeStruct((sa, d), BF16), jax.ShapeDtypeStruct((sa, d), F32)),
        grid=(sa // ROW_TILE,),
        in_specs=[pl.BlockSpec((ROW_TILE, d), lambda i: (0, 0)),
                  pl.BlockSpec((ROW_TILE, d), lambda i: (jnp.maximum(i - 1, 0), 0)),
                  pl.BlockSpec((1, d), lambda i: (0, 0)),
                  pl.BlockSpec((1, 1, 3 * d), lambda i: (jnp.where(i == 0, 1, 0), 0, 0))],
        out_specs=(pl.BlockSpec((ROW_TILE, d), lambda i: (i, 0)),
                   pl.BlockSpec((ROW_TILE, d), lambda i: (i, 0))),
        scratch_shapes=[pltpu.VMEM((1, d), F32), pltpu.VMEM((1, d), F32)],
        compiler_params=_cparams(("arbitrary",), VMEM_LIMIT),
        name="norm_mod_first",
    )(ctx2, x2, g.reshape(1, d), mod3)


def _final_norm_kernel(x_ref, g_ref, o_ref):
    x = x_ref[...]
    o_ref[...] = x * lax.rsqrt(jnp.mean(x * x, axis=-1, keepdims=True) + EPS) * g_ref[...]


def _final_norm(x_all, g, n_ctx_tiles):
    sa, d = x_all.shape
    s = sa - n_ctx_tiles * ROW_TILE
    return pl.pallas_call(
        _final_norm_kernel,
        out_shape=jax.ShapeDtypeStruct((s, d), F32),
        grid=(s // ROW_TILE,),
        in_specs=[pl.BlockSpec((ROW_TILE, d), lambda i: (i + n_ctx_tiles, 0)),
                  pl.BlockSpec((1, d), lambda i: (0, 0))],
        out_specs=pl.BlockSpec((ROW_TILE, d), lambda i: (i, 0)),
        compiler_params=_cparams(("arbitrary",), VMEM_LIMIT),
        name="final_norm",
    )(x_all, g.reshape(1, d))


def _matmul_kernel(a_ref, b_ref, o_ref):
    o_ref[...] = jnp.dot(a_ref[...], b_ref[0].astype(BF16), preferred_element_type=F32).astype(o_ref.dtype)


def _row_tile(sa, cap):
    best = ROW_TILE
    for t in range(ROW_TILE, cap + 1, ROW_TILE):
        if sa % t == 0:
            best = t
    return best


def _in_proj(h, w_all, l, first_col):
    sa, d = h.shape
    n = w_all.shape[2]
    tm, tn = _row_tile(sa, 1280), 512
    assert first_col % tn == 0
    rot, n_col_blocks = first_col // tn, n // tn
    return pl.pallas_call(
        _matmul_kernel,
        out_shape=jax.ShapeDtypeStruct((sa, n), BF16),
        grid=(sa // tm, n // tn),
        in_specs=[pl.BlockSpec((tm, d), lambda i, j: (i, 0)),
                  pl.BlockSpec((1, d, tn), lambda i, j: (l, 0, (j + rot) % n_col_blocks))],
        out_specs=pl.BlockSpec((tm, tn), lambda i, j: (i, j)),
        compiler_params=_cparams(("arbitrary", "arbitrary"), VMEM_LIMIT),
        name="in_proj",
    )(h, w_all)


SUB = 8


def _shift_rows(x3, d, edge, up):
    g = x3.shape[0]
    rows = lax.broadcasted_iota(jnp.int32, x3.shape, 1)
    if up:
        rolled = pltpu.roll(jnp.concatenate([x3, edge[None]], axis=0), SUB - d, 1)
        return jnp.where(rows < SUB - d, rolled[:g], rolled[1:])
    rolled = pltpu.roll(jnp.concatenate([edge[None], x3], axis=0), d, 1)
    return jnp.where(rows >= d, rolled[1:], rolled[:g])


def _conv_kernel(val_ref, glu_ref, gate_ref, vb_ref, gb_ref, va_ref, ga_ref,
                 cw_ref, cb_ref, lng_ref, lnb_ref, wpw_ref, bpw_ref, o_ref, u_scr, *, t):
    i = pl.program_id(0)
    n = pl.num_programs(0)

    def glu(v_ref, g_ref):
        return v_ref[...].astype(F32) * _sigmoid(g_ref[...].astype(F32))

    has_before = jnp.where(i >= 2, 1.0, 0.0)
    has_after = jnp.where((i >= 1) & (i < n - 1), 1.0, 0.0)
    u_scr[0:HALO, :] = glu(vb_ref, gb_ref) * has_before
    u_scr[HALO:HALO + t, :] = glu(val_ref, glu_ref)
    u_scr[HALO + t:HALO + t + HALO, :] = glu(va_ref, ga_ref) * has_after

    first = HALO - CONV_K // 2
    groups = t // SUB
    blocks = []
    for lb in range(u_scr.shape[1] // LANES):
        lanes = slice(lb * LANES, (lb + 1) * LANES)
        acc = None
        for res in range(SUB):
            g = None
            for off in range(res, first + CONV_K, SUB):
                if off < first:
                    continue
                term = cw_ref[off - first:off - first + 1, lanes] * u_scr[pl.ds(off - res, t + SUB), lanes]
                g = term if g is None else g + term
            g = g.reshape(groups + 1, SUB, LANES)
            g = _shift_rows(g[:groups], res, g[groups], up=True) if res else g[:groups]
            acc = g if acc is None else acc + g
        blocks.append(acc.reshape(t, LANES) + cb_ref[:, lanes])
    acc = jnp.concatenate(blocks, axis=1)

    mu = jnp.mean(acc, axis=-1, keepdims=True)
    cen = acc - mu
    var = jnp.mean(cen * cen, axis=-1, keepdims=True)
    y = _silu(cen * lax.rsqrt(var + EPS) * lng_ref[...] + lnb_ref[...])
    out = jnp.dot(y.astype(BF16), wpw_ref[...], preferred_element_type=F32) + bpw_ref[...]
    o_ref[...] = (out * _silu(gate_ref[...].astype(F32))).astype(o_ref.dtype)


def _conv_mixer(z, cb0, cw, cb, lng, lnb, wpw, bpw):
    sa = z.shape[0]
    c = cw.shape[1]
    t = ROW_TILE
    hb = t // HALO
    last_h = sa // HALO - 1
    cw_p = jnp.zeros((32, c), F32).at[:CONV_K].set(cw)
    row = lambda v: v.reshape(1, c)
    before = lambda col: pl.BlockSpec((HALO, c), lambda i: (jnp.maximum(i * hb - 1, 0), col))
    after = lambda col: pl.BlockSpec((HALO, c), lambda i: (jnp.minimum((i + 1) * hb, last_h), col))
    full = lambda shape: pl.BlockSpec(shape, lambda i: (0, 0))
    return pl.pallas_call(
        functools.partial(_conv_kernel, t=t),
        out_shape=jax.ShapeDtypeStruct((sa, c), BF16),
        grid=(sa // t,),
        in_specs=[pl.BlockSpec((t, c), lambda i: (i, cb0)),
                  pl.BlockSpec((t, c), lambda i: (i, cb0 + 1)),
                  pl.BlockSpec((t, c), lambda i: (i, cb0 + 2)),
                  before(cb0), before(cb0 + 1), after(cb0), after(cb0 + 1),
                  full((32, c)), full((1, c)), full((1, c)), full((1, c)),
                  full((c, c)), full((1, c))],
        out_specs=pl.BlockSpec((t, c), lambda i: (i, 0)),
        scratch_shapes=[pltpu.VMEM((t + 2 * HALO, c), F32)],
        compiler_params=_cparams(("arbitrary",), VMEM_LIMIT),
        name="conv_mixer",
    )(z, z, z, z, z, z, z, cw_p, row(cb), row(lng), row(lnb), wpw, row(bpw))


def _scan_tile(a, b, h_in, reverse):
    t, c = a.shape
    groups = t // SUB
    a = a.reshape(groups, SUB, c)
    b = b.reshape(groups, SUB, c)
    rows = lax.broadcasted_iota(jnp.int32, a.shape, 1)
    s = 1
    while s < SUB:
        if reverse:
            keep = rows < SUB - s
            shift = SUB - s
        else:
            keep = rows >= s
            shift = s
        a_sh = jnp.where(keep, pltpu.roll(a, shift, 1), 1.0)
        b_sh = jnp.where(keep, pltpu.roll(b, shift, 1), 0.0)
        b = a * b_sh + b
        a = a * a_sh
        s *= 2
    out = [None] * groups
    h = h_in
    for g in (range(groups - 1, -1, -1) if reverse else range(groups)):
        hg = a[g] * h + b[g]
        out[g] = hg
        h = hg[0:1, :] if reverse else hg[SUB - 1:SUB, :]
    return jnp.concatenate(out, axis=0), h


def _rglru_kernel(*refs, t, reverse):
    if reverse:
        (x_ref, xh_ref, hf_ref, gate_ref, cw_ref, cb_ref, wri_ref, br_ref, bi_ref, lam_ref,
         o_ref, carry_scr) = refs
    else:
        (x_ref, xh_ref, cw_ref, cb_ref, wri_ref, br_ref, bi_ref, lam_ref,
         o_ref, carry_scr) = refs
    s = pl.program_id(0)

    @pl.when(s == 0)
    def _():
        carry_scr[...] = jnp.zeros_like(carry_scr)

    c = x_ref.shape[1]
    has_halo = jnp.where(s >= 2, 1.0, 0.0)
    x3 = x_ref[...].astype(F32).reshape(t // SUB, SUB, c)
    halo = xh_ref[...].astype(F32) * has_halo
    edge = halo[0:SUB] if reverse else halo[HALO - SUB:HALO]
    xc = None
    for k in range(REC_CONV_K):
        d = k if reverse else REC_CONV_K - 1 - k
        term = cw_ref[k:k + 1, :] * (_shift_rows(x3, d, edge, up=reverse) if d else x3)
        xc = term if xc is None else xc + term
    xc = xc.reshape(t, c) + cb_ref[...]

    xcb = xc.astype(BF16)
    blk = c // REC_HEADS
    r_parts, i_parts = [], []
    for hh in range(REC_HEADS):
        ri = jnp.dot(xcb[:, hh * blk:(hh + 1) * blk], wri_ref[hh], preferred_element_type=F32)
        r_parts.append(ri[:, :blk])
        i_parts.append(ri[:, blk:])
    r = _sigmoid(jnp.concatenate(r_parts, axis=1) + br_ref[...])
    ig = _sigmoid(jnp.concatenate(i_parts, axis=1) + bi_ref[...])

    lam = lam_ref[...]
    softplus_neg_lam = jnp.maximum(-lam, 0.0) + jnp.log(1.0 + jnp.exp(-jnp.abs(lam)))
    a = jnp.exp2(r * (-REC_C * LOG2E * softplus_neg_lam))
    one_m_a2 = 1.0 - a * a
    b = (one_m_a2 * lax.rsqrt(jnp.maximum(one_m_a2, 1e-30))) * (ig * xc)

    h, h_out = _scan_tile(a, b, carry_scr[0:1, :], reverse)
    carry_scr[0:1, :] = h_out

    if reverse:
        y = hf_ref[...].astype(F32) + h
        o_ref[...] = (y * _silu(gate_ref[...].astype(F32))).astype(o_ref.dtype)
    else:
        o_ref[...] = h.astype(o_ref.dtype)


def _rglru(z, cb0, hf, cw, cb, w_r, w_i, b_r, b_i, lam, reverse):
    sa = z.shape[0]
    c = cw.shape[1]
    t = ROW_TILE
    n = sa // t
    hb = t // HALO
    last_h = sa // HALO - 1
    wri = jnp.concatenate([w_r, w_i], axis=-1).astype(BF16)
    cw_p = jnp.zeros((8, c), F32).at[:REC_CONV_K].set(cw)
    row = lambda v: v.reshape(1, c)
    full = lambda shape: pl.BlockSpec(shape, lambda s: (0,) * len(shape))
    if reverse:
        tile = lambda s: jnp.where(s == 0, 0, n - s)
        halo_spec = pl.BlockSpec((HALO, c), lambda s: (jnp.minimum((tile(s) + 1) * hb, last_h), cb0))
    else:
        tile = lambda s: s
        halo_spec = pl.BlockSpec((HALO, c), lambda s: (jnp.maximum(tile(s) * hb - 1, 0), cb0))
    in_specs = [pl.BlockSpec((t, c), lambda s: (tile(s), cb0)), halo_spec]
    args = [z, z]
    if reverse:
        in_specs += [pl.BlockSpec((t, c), lambda s: (tile(s), 0)),
                     pl.BlockSpec((t, c), lambda s: (tile(s), cb0 + 1))]
        args += [hf, z]
    in_specs += [full((8, c)), full((1, c)), full(wri.shape), full((1, c)), full((1, c)), full((1, c))]
    args += [cw_p, row(cb), wri, row(b_r), row(b_i), row(lam)]
    return pl.pallas_call(
        functools.partial(_rglru_kernel, t=t, reverse=reverse),
        out_shape=jax.ShapeDtypeStruct((sa, c), BF16),
        grid=(n,),
        in_specs=in_specs,
        out_specs=pl.BlockSpec((t, c), lambda s: (tile(s), 0)),
        scratch_shapes=[pltpu.VMEM((8, c), F32)],
        compiler_params=_cparams(("arbitrary",), VMEM_LIMIT),
        name="rglru_bwd" if reverse else "rglru_fwd",
    )(*args)


def _row_validity(rows):
    r4 = ATT_ROWS
    n_blocks = rows // r4
    assert rows % r4 == 0 and rows >= 3 * r4 and rows >= WIN_H

    def for_block(b):
        rq = np.arange(r4)[:, None]
        jr = np.arange(3 * r4)[None, :]
        r = r4 * b + rq
        rk = r4 * b - r4 + jr
        rs = np.clip(r - WIN_H // 2, 0, rows - WIN_H)
        return (rk >= rs) & (rk < rs + WIN_H) & (rk >= 0) & (rk < rows)

    return np.stack([for_block(0), for_block(1), for_block(n_blocks - 1), np.zeros((r4, 3 * r4), bool)])


def _table_kernel(rp_ref, cm_ref, o_ref, *, row_ok):
    w = GRID_W
    n_dr = 2 * WIN_H - 1
    lane = lax.broadcasted_iota(jnp.int32, (w, 2 * w), 1)
    left = lane < w
    cm = cm_ref[...]
    lo, hi = [], []
    for dr in range(n_dr):
        row = jnp.broadcast_to(rp_ref[0, dr:dr + 1, :], (w, 2 * w))
        lo.append(pltpu.roll(row, 0, 1, stride=1, stride_axis=0) + cm)
        hi.append(pltpu.roll(row, w, 1, stride=1, stride_axis=0) + cm)
    neg = jnp.full((w, 2 * w), NEG, F32)
    n_var, r4, n_jr = row_ok.shape
    for v in range(n_var):
        for rq in range(r4):
            for m in range(n_jr // 2):
                dr = 2 * m - rq + (WIN_H - 1) - r4
                ok_a, ok_b = bool(row_ok[v, rq, 2 * m]), bool(row_ok[v, rq, 2 * m + 1])
                if ok_a and ok_b:
                    tile = jnp.where(left, lo[dr], hi[dr + 1])
                elif ok_a:
                    tile = jnp.where(left, lo[dr], neg)
                elif ok_b:
                    tile = jnp.where(left, neg, hi[dr + 1])
                else:
                    tile = neg
                o_ref[v, 0, rq * w:(rq + 1) * w, m * 2 * w:(m + 1) * 2 * w] = tile


def _attn_tables(rpb, rows):
    n_heads = rpb.shape[0]
    w = GRID_W
    row_ok = _row_validity(rows)
    cq = np.arange(w)[:, None]
    ck = np.arange(w)[None, :]
    cs = np.clip(cq - WIN_W // 2, 0, w - WIN_W)
    col_mask = np.where((ck >= cs) & (ck < cs + WIN_W), 0.0, NEG).astype(np.float32)
    col_mask = np.concatenate([col_mask, col_mask], axis=1)
    rp = jnp.concatenate([rpb[..., WIN_W - 1:], jnp.zeros(rpb.shape[:2] + (2 * w - (2 * WIN_W - 1),), F32),
                          rpb[..., :WIN_W - 1]], axis=-1) * LOG2E
    rp = jnp.pad(rp, ((0, 0), (0, 1), (0, 0)))
    return pl.pallas_call(
        functools.partial(_table_kernel, row_ok=row_ok),
        out_shape=jax.ShapeDtypeStruct((row_ok.shape[0], n_heads, ROW_TILE, 3 * ROW_TILE), F32),
        grid=(n_heads,),
        in_specs=[pl.BlockSpec((1, 16, 2 * w), lambda h: (h, 0, 0)),
                  pl.BlockSpec((w, 2 * w), lambda h: (0, 0))],
        out_specs=pl.BlockSpec((row_ok.shape[0], 1, ROW_TILE, 3 * ROW_TILE), lambda h: (0, h, 0, 0)),
        compiler_params=_cparams(("arbitrary",), VMEM_LIMIT),
        name="attn_tables",
    )(rp, jnp.asarray(col_mask))


def _attn_kernel(q_ref, k0_ref, k1_ref, k2_ref, v0_ref, v1_ref, v2_ref, kc_ref, vc_ref, g_ref,
                 t_ref, o_ref, *, heads, scale):
    nt = (((1,), (1,)), ((), ()))
    t = q_ref.shape[0]
    ones_loc = jnp.ones((3 * t, HEAD_DIM), BF16)
    ones_ctx = jnp.ones((kc_ref.shape[0], HEAD_DIM), BF16)

    def scores(h):
        sl = slice(h * HEAD_DIM, (h + 1) * HEAD_DIM)
        q = (q_ref[:, sl].astype(F32) * (scale * LOG2E)).astype(BF16)
        k = jnp.concatenate([k0_ref[:, sl], k1_ref[:, sl], k2_ref[:, sl]], axis=0)
        s_loc = lax.dot_general(q, k, nt, preferred_element_type=F32) + t_ref[0, h]
        s_ctx = lax.dot_general(q, kc_ref[:, sl], nt, preferred_element_type=F32)
        return s_loc, s_ctx

    nxt = scores(0)
    for h in range(heads):
        sl = slice(h * HEAD_DIM, (h + 1) * HEAD_DIM)
        s_loc, s_ctx = nxt
        if h + 1 < heads:
            nxt = scores(h + 1)
        v = jnp.concatenate([v0_ref[:, sl], v1_ref[:, sl], v2_ref[:, sl]], axis=0)
        m = jnp.maximum(jnp.max(s_loc, axis=-1, keepdims=True), jnp.max(s_ctx, axis=-1, keepdims=True))
        p_loc = jnp.exp2(s_loc - m).astype(BF16)
        p_ctx = jnp.exp2(s_ctx - m).astype(BF16)
        od = (jnp.dot(p_loc, jnp.concatenate([v, ones_loc], axis=1), preferred_element_type=F32)
              + jnp.dot(p_ctx, jnp.concatenate([vc_ref[:, sl], ones_ctx], axis=1), preferred_element_type=F32))
        o = od[:, :HEAD_DIM] / od[:, HEAD_DIM:]
        o_ref[:, sl] = (o * _silu(g_ref[:, sl].astype(F32))).astype(o_ref.dtype)


def _attention(z, tables, w_att, col0):
    sa = z.shape[0]
    t = ROW_TILE
    nb = sa // t - 1
    hg = ATT_HEADS_PER_STEP
    wl = hg * HEAD_DIM
    n_heads = w_att // HEAD_DIM
    assert all(c % wl == 0 for c in col0)
    cq, ckk, cv, cg = (c // wl for c in col0)
    clampb = lambda s: jnp.clip(s, 1, nb)
    variant = lambda s: jnp.where(s == 0, 3, jnp.where(s == 1, 0, jnp.where(s == nb, 2, 1)))
    blk = lambda rowf, colb: pl.BlockSpec((t, wl), lambda g, s: (rowf(s), colb + g))
    return pl.pallas_call(
        functools.partial(_attn_kernel, heads=hg, scale=HEAD_DIM ** -0.5),
        out_shape=jax.ShapeDtypeStruct((sa, w_att), BF16),
        grid=(n_heads // hg, nb + 1),
        in_specs=[blk(lambda s: s, cq),
                  blk(lambda s: clampb(s - 1), ckk), blk(lambda s: clampb(s), ckk), blk(lambda s: clampb(s + 1), ckk),
                  blk(lambda s: clampb(s - 1), cv), blk(lambda s: clampb(s), cv), blk(lambda s: clampb(s + 1), cv),
                  blk(lambda s: 0, ckk), blk(lambda s: 0, cv),
                  blk(lambda s: s, cg),
                  pl.BlockSpec((1, hg, t, 3 * t), lambda g, s: (variant(s), g, 0, 0))],
        out_specs=pl.BlockSpec((t, wl), lambda g, s: (s, g)),
        compiler_params=_cparams(("arbitrary", "arbitrary"), VMEM_LIMIT),
        name="nbr_attention",
    )(z, z, z, z, z, z, z, z, z, z, tables)


def _out_proj_kernel(a_ref, b_ref, c_ref, wa_ref, wb_ref, wc_ref, x_ref, g_ref, o_ref, *, tm, n_ctx):
    i = pl.program_id(0)
    acc = (jnp.dot(a_ref[...], wa_ref[0].astype(BF16), preferred_element_type=F32)
           + jnp.dot(b_ref[...], wb_ref[0].astype(BF16), preferred_element_type=F32)
           + jnp.dot(c_ref[...], wc_ref[0].astype(BF16), preferred_element_type=F32))
    rows = i * tm + lax.broadcasted_iota(jnp.int32, acc.shape, 0)
    g = jnp.where(rows < n_ctx, g_ref[1:2, :], g_ref[0:1, :])
    o_ref[...] = x_ref[...] + g * acc


def _out_proj(mix_a, mix_b, mix_c, w_all, l, x_all, mod, n_ctx):
    sa, d = x_all.shape
    wa, wb, wc = mix_a.shape[1], mix_b.shape[1], mix_c.shape[1]
    assert wa == wb and wc % wa == 0
    tm, tn = _row_tile(sa, 1280), 512
    gate_blk0 = 2 * d // tn
    return pl.pallas_call(
        functools.partial(_out_proj_kernel, tm=tm, n_ctx=n_ctx),
        out_shape=jax.ShapeDtypeStruct((sa, d), F32),
        grid=(sa // tm, d // tn),
        in_specs=[pl.BlockSpec((tm, wa), lambda i, j: (i, 0)),
                  pl.BlockSpec((tm, wb), lambda i, j: (i, 0)),
                  pl.BlockSpec((tm, wc), lambda i, j: (i, 0)),
                  pl.BlockSpec((1, wa, tn), lambda i, j: (l, 0, j)),
                  pl.BlockSpec((1, wb, tn), lambda i, j: (l, 1, j)),
                  pl.BlockSpec((1, wc, tn), lambda i, j: (l, (wa + wb) // wc, j)),
                  pl.BlockSpec((tm, tn), lambda i, j: (i, j)),
                  pl.BlockSpec((8, tn), lambda i, j: (0, gate_blk0 + j))],
        out_specs=pl.BlockSpec((tm, tn), lambda i, j: (i, j)),
        compiler_params=_cparams(("arbitrary", "arbitrary"), VMEM_LIMIT),
        name="out_proj",
    )(mix_a, mix_b, mix_c, w_all, w_all, w_all, x_all, mod)


def kernel(x, c, ctx, c_ctx, w_ada, b_ada, norm_g, w_in, conv_w, conv_b, ln_g, ln_b, w_pw, b_pw,
           rconv_w, rconv_b, w_r, b_r, w_i, b_i, lam, rpb, w_out, final_g):
    bsz, s, d = x.shape
    n_ctx = ctx.shape[1]
    depth = w_ada.shape[0]
    w_conv = conv_w.shape[2]
    w_rec = rconv_w.shape[3]
    w_att = rpb.shape[1] * HEAD_DIM
    assert bsz == 1 and n_ctx == ROW_TILE and s % ROW_TILE == 0 and ATT_ROWS * GRID_W == ROW_TILE
    assert w_conv == w_rec == 1024
    rows = s // GRID_W
    first_col = 3 * w_conv + 2 * w_rec
    col0 = (0, w_att, 2 * w_att, 3 * w_att)
    conv_cb0 = 4 * w_att // w_conv
    rec_cb0 = conv_cb0 + 3

    cc = jnp.zeros((8, d), F32).at[0].set(c[0]).at[1].set(c_ctx)
    mods = _ada_all_layers(cc, w_ada, b_ada)

    for l in range(depth):
        mod = mods[l]
        if l == 0:
            h, x_all = _norm_mod_first(ctx[0], x[0], norm_g[l], mod.reshape(8, 1, 3 * d))
        else:
            h = _norm_mod(x_all, norm_g[l], mod.reshape(8, 1, 3 * d))
        z = _in_proj(h, w_in, l, first_col)
        mix_a = _conv_mixer(z, conv_cb0, conv_w[l], conv_b[l], ln_g[l], ln_b[l], w_pw[l].astype(BF16), b_pw[l])
        hf = _rglru(z, rec_cb0, None, rconv_w[l, 0], rconv_b[l, 0], w_r[l, 0], w_i[l, 0], b_r[l, 0],
                    b_i[l, 0], lam[l, 0], reverse=False)
        mix_b = _rglru(z, rec_cb0, hf, rconv_w[l, 1], rconv_b[l, 1], w_r[l, 1], w_i[l, 1], b_r[l, 1],
                       b_i[l, 1], lam[l, 1], reverse=True)
        mix_c = _attention(z, _attn_tables(rpb[l], rows), w_att, col0)
        x_all = _out_proj(mix_a, mix_b, mix_c, w_out, l, x_all, mod, n_ctx)

    return _final_norm(x_all, final_g, n_ctx // ROW_TILE)[None]
```

```python
import functools

import numpy as np
import jax
import jax.numpy as jnp
from jax import lax
from jax.experimental import pallas as pl
from jax.experimental.pallas import tpu as pltpu

F32 = jnp.float32
BF16 = jnp.bfloat16

GRID_W = 64
WIN_H = 8
WIN_W = 16
CONV_K = 31
REC_CONV_K = 4
REC_HEADS = 8
REC_C = 8.0
HEAD_DIM = 128
EPS = 1e-6
NEG = -1e30
LOG2E = 1.4426950408889634
LANES = 128

ROW_TILE = 256
HALO = 16
ATT_ROWS = 4
ATT_HEADS_PER_STEP = 16
VMEM_LIMIT = 56 * 1024 * 1024


def _cparams(sem, vmem=None):
    return pltpu.CompilerParams(dimension_semantics=sem, vmem_limit_bytes=vmem)


def _sigmoid(x):
    return 1.0 / (1.0 + jnp.exp2(x * -LOG2E))


def _silu(x):
    return x * _sigmoid(x)


def _ada_kernel(cc_ref, w_ref, b_ref, o_ref):
    a = _silu(cc_ref[...]).astype(BF16)
    w = w_ref[0].astype(BF16)
    o_ref[0] = jnp.dot(a, w, preferred_element_type=F32) + b_ref[0]


def _ada_all_layers(cc, w_ada, b_ada):
    depth, d, n = w_ada.shape
    tn = 512
    return pl.pallas_call(
        _ada_kernel,
        out_shape=jax.ShapeDtypeStruct((depth, 8, n), F32),
        grid=(depth, n // tn),
        in_specs=[pl.BlockSpec((8, d), lambda l, j: (0, 0)),
                  pl.BlockSpec((1, d, tn), lambda l, j: (l, 0, j)),
                  pl.BlockSpec((1, 1, tn), lambda l, j: (l, 0, j))],
        out_specs=pl.BlockSpec((1, 8, tn), lambda l, j: (l, 0, j)),
        compiler_params=_cparams(("arbitrary", "arbitrary"), VMEM_LIMIT),
        name="ada_mod",
    )(cc, w_ada, b_ada.reshape(depth, 1, n))


NORM_ROWS = 16


def _norm_mod_rows(src_ref, g_ref, mod_ref, h_ref, sc_scr, sh_scr, copy_ref, *, d):
    mod = mod_ref[0]
    sc_scr[...] = g_ref[...] * (1.0 + mod[:, d:2 * d])
    sh_scr[...] = mod[:, :d]

    def trip(r, carry):
        rows = pl.ds(pl.multiple_of(r * NORM_ROWS, NORM_ROWS), NORM_ROWS)
        x = src_ref[rows, :]
        if copy_ref is not None:
            copy_ref[rows, :] = x
        rs = lax.rsqrt(jnp.mean(x * x, axis=-1, keepdims=True) + EPS)
        h_ref[rows, :] = (src_ref[rows, :] * rs * sc_scr[...] + sh_scr[...]).astype(h_ref.dtype)
        return carry

    lax.fori_loop(0, src_ref.shape[0] // NORM_ROWS, trip, 0, unroll=4)


def _norm_mod_kernel(x_ref, g_ref, mod_ref, o_ref, sc_scr, sh_scr, *, d):
    _norm_mod_rows(x_ref, g_ref, mod_ref, o_ref, sc_scr, sh_scr, None, d=d)


def _norm_mod(x_all, g, mod3):
    sa, d = x_all.shape
    return pl.pallas_call(
        functools.partial(_norm_mod_kernel, d=d),
        out_shape=jax.ShapeDtypeStruct((sa, d), BF16),
        grid=(sa // ROW_TILE,),
        in_specs=[pl.BlockSpec((ROW_TILE, d), lambda i: (i, 0)),
                  pl.BlockSpec((1, d), lambda i: (0, 0)),
                  pl.BlockSpec((1, 1, 3 * d), lambda i: (jnp.where(i == 0, 1, 0), 0, 0))],
        out_specs=pl.BlockSpec((ROW_TILE, d), lambda i: (i, 0)),
        scratch_shapes=[pltpu.VMEM((1, d), F32), pltpu.VMEM((1, d), F32)],
        compiler_params=_cparams(("arbitrary",), VMEM_LIMIT),
        name="norm_mod",
    )(x_all, g.reshape(1, d), mod3)


def _norm_mod_first_kernel(ctx_ref, x_ref, g_ref, mod_ref, h_ref, xall_ref, sc_scr, sh_scr, *, d):
    emit = lambda src_ref: _norm_mod_rows(src_ref, g_ref, mod_ref, h_ref, sc_scr, sh_scr, xall_ref, d=d)
    pl.when(pl.program_id(0) == 0)(lambda: emit(ctx_ref))
    pl.when(pl.program_id(0) > 0)(lambda: emit(x_ref))


def _norm_mod_first(ctx2, x2, g, mod3):
    n_ctx, d = ctx2.shape
    sa = n_ctx + x2.shape[0]
    assert n_ctx == ROW_TILE
    return pl.pallas_call(
        functools.partial(_norm_mod_first_kernel, d=d),
        out_shape=(jax.ShapeDtypeStruct((sa, d), BF16), jax.ShapeDtypeStruct((sa, d), F32)),
        grid=(sa // ROW_TILE,),
        in_specs=[pl.BlockSpec((ROW_TILE, d), lambda i: (0, 0)),
                  pl.BlockSpec((ROW_TILE, d), lambda i: (jnp.maximum(i - 1, 0), 0)),
                  pl.BlockSpec((1, d), lambda i: (0, 0)),
                  pl.BlockSpec((1, 1, 3 * d), lambda i: (jnp.where(i == 0, 1, 0), 0, 0))],
        out_specs=(pl.BlockSpec((ROW_TILE, d), lambda i: (i, 0)),
                   pl.BlockSpec((ROW_TILE, d), lambda i: (i, 0))),
        scratch_shapes=[pltpu.VMEM((1, d), F32), pltpu.VMEM((1, d), F32)],
        compiler_params=_cparams(("arbitrary",), VMEM_LIMIT),
        name="norm_mod_first",
    )(ctx2, x2, g.reshape(1, d), mod3)


def _final_norm_kernel(x_ref, g_ref, o_ref):
    x = x_ref[...]
    o_ref[...] = x * lax.rsqrt(jnp.mean(x * x, axis=-1, keepdims=True) + EPS) * g_ref[...]


def _final_norm(x_all, g, n_ctx_tiles):
    sa, d = x_all.shape
    s = sa - n_ctx_tiles * ROW_TILE
    return pl.pallas_call(
        _final_norm_kernel,
        out_shape=jax.ShapeDtypeStruct((s, d), F32),
        grid=(s // ROW_TILE,),
        in_specs=[pl.BlockSpec((ROW_TILE, d), lambda i: (i + n_ctx_tiles, 0)),
                  pl.BlockSpec((1, d), lambda i: (0, 0))],
        out_specs=pl.BlockSpec((ROW_TILE, d), lambda i: (i, 0)),
        compiler_params=_cparams(("arbitrary",), VMEM_LIMIT),
        name="final_norm",
    )(x_all, g.reshape(1, d))


def _matmul_kernel(a_ref, b_ref, o_ref):
    o_ref[...] = jnp.dot(a_ref[...], b_ref[0].astype(BF16), preferred_element_type=F32).astype(o_ref.dtype)


def _row_tile(sa, cap):
    best = ROW_TILE
    for t in range(ROW_TILE, cap + 1, ROW_TILE):
        if sa % t == 0:
            best = t
    return best


def _in_proj(h, w_all, l, first_col):
    sa, d = h.shape
    n = w_all.shape[2]
    tm, tn = _row_tile(sa, 1280), 512
    assert first_col % tn == 0
    rot, n_col_blocks = first_col // tn, n // tn
    return pl.pallas_call(
        _matmul_kernel,
        out_shape=jax.ShapeDtypeStruct((sa, n), BF16),
        grid=(sa // tm, n // tn),
        in_specs=[pl.BlockSpec((tm, d), lambda i, j: (i, 0)),
                  pl.BlockSpec((1, d, tn), lambda i, j: (l, 0, (j + rot) % n_col_blocks))],
        out_specs=pl.BlockSpec((tm, tn), lambda i, j: (i, j)),
        compiler_params=_cparams(("arbitrary", "arbitrary"), VMEM_LIMIT),
        name="in_proj",
    )(h, w_all)


SUB = 8


def _shift_rows(x3, d, edge, up):
    rows = lax.broadcasted_iota(jnp.int32, x3.shape, 1)
    if up:
        after = jnp.concatenate([x3[1:], edge[None]], axis=0)
        return pltpu.roll(jnp.where(rows >= d, x3, after), SUB - d, 1)
    before = jnp.concatenate([edge[None], x3[:-1]], axis=0)
    return pltpu.roll(jnp.where(rows < SUB - d, x3, before), d, 1)


def _conv_kernel(val_ref, glu_ref, gate_ref, vb_ref, gb_ref, va_ref, ga_ref,
                 cw_ref, cb_ref, lng_ref, lnb_ref, wpw_ref, bpw_ref, o_ref, u_scr, *, t):
    i = pl.program_id(0)
    n = pl.num_programs(0)

    def glu(v_ref, g_ref):
        return v_ref[...].astype(F32) * _sigmoid(g_ref[...].astype(F32))

    has_before = jnp.where(i >= 2, 1.0, 0.0)
    has_after = jnp.where((i >= 1) & (i < n - 1), 1.0, 0.0)
    u_scr[0:HALO, :] = glu(vb_ref, gb_ref) * has_before
    u_scr[HALO:HALO + t, :] = glu(val_ref, glu_ref)
    u_scr[HALO + t:HALO + t + HALO, :] = glu(va_ref, ga_ref) * has_after

    first = HALO - CONV_K // 2
    groups = t // SUB
    blocks = []
    for lb in range(u_scr.shape[1] // LANES):
        lanes = slice(lb * LANES, (lb + 1) * LANES)
        acc = None
        for res in range(SUB):
            g = None
            for off in range(res, first + CONV_K, SUB):
                if off < first:
                    continue
                term = cw_ref[off - first:off - first + 1, lanes] * u_scr[pl.ds(off - res, t + SUB), lanes]
                g = term if g is None else g + term
            g = g.reshape(groups + 1, SUB, LANES)
            g = _shift_rows(g[:groups], res, g[groups], up=True) if res else g[:groups]
            acc = g if acc is None else acc + g
        blocks.append(acc.reshape(t, LANES) + cb_ref[:, lanes])
    acc = jnp.concatenate(blocks, axis=1)

    mu = jnp.mean(acc, axis=-1, keepdims=True)
    cen = acc - mu
    var = jnp.mean(cen * cen, axis=-1, keepdims=True)
    y = _silu(cen * lax.rsqrt(var + EPS) * lng_ref[...] + lnb_ref[...])
    out = jnp.dot(y.astype(BF16), wpw_ref[...], preferred_element_type=F32) + bpw_ref[...]
    o_ref[...] = (out * _silu(gate_ref[...].astype(F32))).astype(o_ref.dtype)


def _conv_mixer(z, cb0, cw, cb, lng, lnb, wpw, bpw):
    sa = z.shape[0]
    c = cw.shape[1]
    t = ROW_TILE
    hb = t // HALO
    last_h = sa // HALO - 1
    cw_p = jnp.zeros((32, c), F32).at[:CONV_K].set(cw)
    row = lambda v: v.reshape(1, c)
    before = lambda col: pl.BlockSpec((HALO, c), lambda i: (jnp.maximum(i * hb - 1, 0), col))
    after = lambda col: pl.BlockSpec((HALO, c), lambda i: (jnp.minimum((i + 1) * hb, last_h), col))
    full = lambda shape: pl.BlockSpec(shape, lambda i: (0, 0))
    return pl.pallas_call(
        functools.partial(_conv_kernel, t=t),
        out_shape=jax.ShapeDtypeStruct((sa, c), BF16),
        grid=(sa // t,),
        in_specs=[pl.BlockSpec((t, c), lambda i: (i, cb0)),
                  pl.BlockSpec((t, c), lambda i: (i, cb0 + 1)),
                  pl.BlockSpec((t, c), lambda i: (i, cb0 + 2)),
                  before(cb0), before(cb0 + 1), after(cb0), after(cb0 + 1),
                  full((32, c)), full((1, c)), full((1, c)), full((1, c)),
                  full((c, c)), full((1, c))],
        out_specs=pl.BlockSpec((t, c), lambda i: (i, 0)),
        scratch_shapes=[pltpu.VMEM((t + 2 * HALO, c), F32)],
        compiler_params=_cparams(("arbitrary",), VMEM_LIMIT),
        name="conv_mixer",
    )(z, z, z, z, z, z, z, cw_p, row(cb), row(lng), row(lnb), wpw, row(bpw))


def _scan_tile(a, b, h_in, reverse):
    t, c = a.shape
    groups = t // SUB
    a = a.reshape(groups, SUB, c)
    b = b.reshape(groups, SUB, c)
    rows = lax.broadcasted_iota(jnp.int32, a.shape, 1)
    s = 1
    while s < SUB:
        if reverse:
            keep = rows < SUB - s
            shift = SUB - s
        else:
            keep = rows >= s
            shift = s
        a_sh = jnp.where(keep, pltpu.roll(a, shift, 1), 1.0)
        b_sh = jnp.where(keep, pltpu.roll(b, shift, 1), 0.0)
        b = a * b_sh + b
        a = a * a_sh
        s *= 2
    out = [None] * groups
    h = h_in
    for g in (range(groups - 1, -1, -1) if reverse else range(groups)):
        hg = a[g] * h + b[g]
        out[g] = hg
        h = hg[0:1, :] if reverse else hg[SUB - 1:SUB, :]
    return jnp.concatenate(out, axis=0), h


def _rglru_kernel(*refs, t, reverse):
    if reverse:
        (x_ref, xh_ref, hf_ref, gate_ref, cw_ref, cb_ref, wri_ref, br_ref, bi_ref, lam_ref,
         o_ref, carry_scr) = refs
    else:
        (x_ref, xh_ref, cw_ref, cb_ref, wri_ref, br_ref, bi_ref, lam_ref,
         o_ref, carry_scr) = refs
    s = pl.program_id(0)

    @pl.when(s == 0)
    def _():
        carry_scr[...] = jnp.zeros_like(carry_scr)

    c = x_ref.shape[1]
    has_halo = jnp.where(s >= 2, 1.0, 0.0)
    x3 = x_ref[...].astype(F32).reshape(t // SUB, SUB, c)
    halo = xh_ref[...].astype(F32) * has_halo
    edge = halo[0:SUB] if reverse else halo[HALO - SUB:HALO]
    xc = None
    for k in range(REC_CONV_K):
        d = k if reverse else REC_CONV_K - 1 - k
        term = cw_ref[k:k + 1, :] * (_shift_rows(x3, d, edge, up=reverse) if d else x3)
        xc = term if xc is None else xc + term
    xc = xc.reshape(t, c) + cb_ref[...]

    xcb = xc.astype(BF16)
    blk = c // REC_HEADS
    r_parts, i_parts = [], []
    for hh in range(REC_HEADS):
        ri = jnp.dot(xcb[:, hh * blk:(hh + 1) * blk], wri_ref[hh], preferred_element_type=F32)
        r_parts.append(ri[:, :blk])
        i_parts.append(ri[:, blk:])
    r = _sigmoid(jnp.concatenate(r_parts, axis=1) + br_ref[...])
    ig = _sigmoid(jnp.concatenate(i_parts, axis=1) + bi_ref[...])

    lam = lam_ref[...]
    softplus_neg_lam = jnp.maximum(-lam, 0.0) + jnp.log(1.0 + jnp.exp(-jnp.abs(lam)))
    a = jnp.exp2(r * (-REC_C * LOG2E * softplus_neg_lam))
    one_m_a2 = 1.0 - a * a
    b = (one_m_a2 * lax.rsqrt(jnp.maximum(one_m_a2, 1e-30))) * (ig * xc)

    h, h_out = _scan_tile(a, b, carry_scr[0:1, :], reverse)
    carry_scr[0:1, :] = h_out

    if reverse:
        y = hf_ref[...].astype(F32) + h
        o_ref[...] = (y * _silu(gate_ref[...].astype(F32))).astype(o_ref.dtype)
    else:
        o_ref[...] = h.astype(o_ref.dtype)


def _rglru(z, cb0, hf, cw, cb, w_r, w_i, b_r, b_i, lam, reverse):
    sa = z.shape[0]
    c = cw.shape[1]
    t = ROW_TILE
    n = sa // t
    hb = t // HALO
    last_h = sa // HALO - 1
    wri = jnp.concatenate([w_r, w_i], axis=-1).astype(BF16)
    cw_p = jnp.zeros((8, c), F32).at[:REC_CONV_K].set(cw)
    row = lambda v: v.reshape(1, c)
    full = lambda shape: pl.BlockSpec(shape, lambda s: (0,) * len(shape))
    if reverse:
        tile = lambda s: jnp.where(s == 0, 0, n - s)
        halo_spec = pl.BlockSpec((HALO, c), lambda s: (jnp.minimum((tile(s) + 1) * hb, last_h), cb0))
    else:
        tile = lambda s: s
        halo_spec = pl.BlockSpec((HALO, c), lambda s: (jnp.maximum(tile(s) * hb - 1, 0), cb0))
    in_specs = [pl.BlockSpec((t, c), lambda s: (tile(s), cb0)), halo_spec]
    args = [z, z]
    if reverse:
        in_specs += [pl.BlockSpec((t, c), lambda s: (tile(s), 0)),
                     pl.BlockSpec((t, c), lambda s: (tile(s), cb0 + 1))]
        args += [hf, z]
    in_specs += [full((8, c)), full((1, c)), full(wri.shape), full((1, c)), full((1, c)), full((1, c))]
    args += [cw_p, row(cb), wri, row(b_r), row(b_i), row(lam)]
    return pl.pallas_call(
        functools.partial(_rglru_kernel, t=t, reverse=reverse),
        out_shape=jax.ShapeDtypeStruct((sa, c), BF16),
        grid=(n,),
        in_specs=in_specs,
        out_specs=pl.BlockSpec((t, c), lambda s: (tile(s), 0)),
        scratch_shapes=[pltpu.VMEM((8, c), F32)],
        compiler_params=_cparams(("arbitrary",), VMEM_LIMIT),
        name="rglru_bwd" if reverse else "rglru_fwd",
    )(*args)


def _row_validity(rows):
    r4 = ATT_ROWS
    n_blocks = rows // r4
    assert rows % r4 == 0 and rows >= 3 * r4 and rows >= WIN_H

    def for_block(b):
        rq = np.arange(r4)[:, None]
        jr = np.arange(3 * r4)[None, :]
        r = r4 * b + rq
        rk = r4 * b - r4 + jr
        rs = np.clip(r - WIN_H // 2, 0, rows - WIN_H)
        return (rk >= rs) & (rk < rs + WIN_H) & (rk >= 0) & (rk < rows)

    return np.stack([for_block(0), for_block(1), for_block(n_blocks - 1), np.zeros((r4, 3 * r4), bool)])


def _table_kernel(rp_ref, cm_ref, o_ref, *, row_ok):
    w = GRID_W
    n_dr = 2 * WIN_H - 1
    lane = lax.broadcasted_iota(jnp.int32, (w, 2 * w), 1)
    left = lane < w
    cm = cm_ref[...]
    lo, hi = [], []
    for dr in range(n_dr):
        row = jnp.broadcast_to(rp_ref[0, dr:dr + 1, :], (w, 2 * w))
        lo.append(pltpu.roll(row, 0, 1, stride=1, stride_axis=0) + cm)
        hi.append(pltpu.roll(row, w, 1, stride=1, stride_axis=0) + cm)
    neg = jnp.full((w, 2 * w), NEG, F32)
    n_var, r4, n_jr = row_ok.shape
    for v in range(n_var):
        for rq in range(r4):
            for m in range(n_jr // 2):
                dr = 2 * m - rq + (WIN_H - 1) - r4
                ok_a, ok_b = bool(row_ok[v, rq, 2 * m]), bool(row_ok[v, rq, 2 * m + 1])
                if ok_a and ok_b:
                    tile = jnp.where(left, lo[dr], hi[dr + 1])
                elif ok_a:
                    tile = jnp.where(left, lo[dr], neg)
                elif ok_b:
                    tile = jnp.where(left, neg, hi[dr + 1])
                else:
                    tile = neg
                o_ref[v, 0, rq * w:(rq + 1) * w, m * 2 * w:(m + 1) * 2 * w] = tile


def _attn_tables(rpb, rows):
    n_heads = rpb.shape[0]
    w = GRID_W
    row_ok = _row_validity(rows)
    cq = np.arange(w)[:, None]
    ck = np.arange(w)[None, :]
    cs = np.clip(cq - WIN_W // 2, 0, w - WIN_W)
    col_mask = np.where((ck >= cs) & (ck < cs + WIN_W), 0.0, NEG).astype(np.float32)
    col_mask = np.concatenate([col_mask, col_mask], axis=1)
    rp = jnp.concatenate([rpb[..., WIN_W - 1:], jnp.zeros(rpb.shape[:2] + (2 * w - (2 * WIN_W - 1),), F32),
                          rpb[..., :WIN_W - 1]], axis=-1) * LOG2E
    rp = jnp.pad(rp, ((0, 0), (0, 1), (0, 0)))
    return pl.pallas_call(
        functools.partial(_table_kernel, row_ok=row_ok),
        out_shape=jax.ShapeDtypeStruct((row_ok.shape[0], n_heads, ROW_TILE, 3 * ROW_TILE), F32),
        grid=(n_heads,),
        in_specs=[pl.BlockSpec((1, 16, 2 * w), lambda h: (h, 0, 0)),
                  pl.BlockSpec((w, 2 * w), lambda h: (0, 0))],
        out_specs=pl.BlockSpec((row_ok.shape[0], 1, ROW_TILE, 3 * ROW_TILE), lambda h: (0, h, 0, 0)),
        compiler_params=_cparams(("arbitrary",), VMEM_LIMIT),
        name="attn_tables",
    )(rp, jnp.asarray(col_mask))


def _attn_kernel(q_ref, k0_ref, k1_ref, k2_ref, v0_ref, v1_ref, v2_ref, kc_ref, vc_ref, g_ref,
                 t_ref, o_ref, *, heads, scale):
    nt = (((1,), (1,)), ((), ()))
    t = q_ref.shape[0]
    ones_loc = jnp.ones((3 * t, HEAD_DIM), BF16)
    ones_ctx = jnp.ones((kc_ref.shape[0], HEAD_DIM), BF16)

    def scores(h):
        sl = slice(h * HEAD_DIM, (h + 1) * HEAD_DIM)
        q = (q_ref[:, sl].astype(F32) * (scale * LOG2E)).astype(BF16)
        k = jnp.concatenate([k0_ref[:, sl], k1_ref[:, sl], k2_ref[:, sl]], axis=0)
        s_loc = lax.dot_general(q, k, nt, preferred_element_type=F32) + t_ref[0, h]
        s_ctx = lax.dot_general(q, kc_ref[:, sl], nt, preferred_element_type=F32)
        return s_loc, s_ctx

    nxt = scores(0)
    for h in range(heads):
        sl = slice(h * HEAD_DIM, (h + 1) * HEAD_DIM)
        s_loc, s_ctx = nxt
        if h + 1 < heads:
            nxt = scores(h + 1)
        v = jnp.concatenate([v0_ref[:, sl], v1_ref[:, sl], v2_ref[:, sl]], axis=0)
        m = jnp.maximum(jnp.max(s_loc, axis=-1, keepdims=True), jnp.max(s_ctx, axis=-1, keepdims=True))
        p_loc = jnp.exp2(s_loc - m).astype(BF16)
        p_ctx = jnp.exp2(s_ctx - m).astype(BF16)
        od = (jnp.dot(p_loc, jnp.concatenate([v, ones_loc], axis=1), preferred_element_type=F32)
              + jnp.dot(p_ctx, jnp.concatenate([vc_ref[:, sl], ones_ctx], axis=1), preferred_element_type=F32))
        o = od[:, :HEAD_DIM] / od[:, HEAD_DIM:]
        o_ref[:, sl] = (o * _silu(g_ref[:, sl].astype(F32))).astype(o_ref.dtype)


def _attention(z, tables, w_att, col0):
    sa = z.shape[0]
    t = ROW_TILE
    nb = sa // t - 1
    hg = ATT_HEADS_PER_STEP
    wl = hg * HEAD_DIM
    n_heads = w_att // HEAD_DIM
    assert all(c % wl == 0 for c in col0)
    cq, ckk, cv, cg = (c // wl for c in col0)
    clampb = lambda s: jnp.clip(s, 1, nb)
    variant = lambda s: jnp.where(s == 0, 3, jnp.where(s == 1, 0, jnp.where(s == nb, 2, 1)))
    blk = lambda rowf, colb: pl.BlockSpec((t, wl), lambda g, s: (rowf(s), colb + g))
    return pl.pallas_call(
        functools.partial(_attn_kernel, heads=hg, scale=HEAD_DIM ** -0.5),
        out_shape=jax.ShapeDtypeStruct((sa, w_att), BF16),
        grid=(n_heads // hg, nb + 1),
        in_specs=[blk(lambda s: s, cq),
                  blk(lambda s: clampb(s - 1), ckk), blk(lambda s: clampb(s), ckk), blk(lambda s: clampb(s + 1), ckk),
                  blk(lambda s: clampb(s - 1), cv), blk(lambda s: clampb(s), cv), blk(lambda s: clampb(s + 1), cv),
                  blk(lambda s: 0, ckk), blk(lambda s: 0, cv),
                  blk(lambda s: s, cg),
                  pl.BlockSpec((1, hg, t, 3 * t), lambda g, s: (variant(s), g, 0, 0))],
        out_specs=pl.BlockSpec((t, wl), lambda g, s: (s, g)),
        compiler_params=_cparams(("arbitrary", "arbitrary"), VMEM_LIMIT),
        name="nbr_attention",
    )(z, z, z, z, z, z, z, z, z, z, tables)


def _out_proj_kernel(a_ref, b_ref, c_ref, wa_ref, wb_ref, wc_ref, x_ref, g_ref, o_ref, *, tm, n_ctx):
    i = pl.program_id(0)
    acc = (jnp.dot(a_ref[...], wa_ref[0].astype(BF16), preferred_element_type=F32)
           + jnp.dot(b_ref[...], wb_ref[0].astype(BF16), preferred_element_type=F32)
           + jnp.dot(c_ref[...], wc_ref[0].astype(BF16), preferred_element_type=F32))
    rows = i * tm + lax.broadcasted_iota(jnp.int32, acc.shape, 0)
    g = jnp.where(rows < n_ctx, g_ref[1:2, :], g_ref[0:1, :])
    o_ref[...] = x_ref[...] + g * acc


def _out_proj(mix_a, mix_b, mix_c, w_all, l, x_all, mod, n_ctx):
    sa, d = x_all.shape
    wa, wb, wc = mix_a.shape[1], mix_b.shape[1], mix_c.shape[1]
    assert wa == wb and wc % wa == 0
    tm, tn = _row_tile(sa, 1280), 512
    gate_blk0 = 2 * d // tn
    return pl.pallas_call(
        functools.partial(_out_proj_kernel, tm=tm, n_ctx=n_ctx),
        out_shape=jax.ShapeDtypeStruct((sa, d), F32),
        grid=(sa // tm, d // tn),
        in_specs=[pl.BlockSpec((tm, wa), lambda i, j: (i, 0)),
                  pl.BlockSpec((tm, wb), lambda i, j: (i, 0)),
                  pl.BlockSpec((tm, wc), lambda i, j: (i, 0)),
                  pl.BlockSpec((1, wa, tn), lambda i, j: (l, 0, j)),
                  pl.BlockSpec((1, wb, tn), lambda i, j: (l, 1, j)),
                  pl.BlockSpec((1, wc, tn), lambda i, j: (l, (wa + wb) // wc, j)),
                  pl.BlockSpec((tm, tn), lambda i, j: (i, j)),
                  pl.BlockSpec((8, tn), lambda i, j: (0, gate_blk0 + j))],
        out_specs=pl.BlockSpec((tm, tn), lambda i, j: (i, j)),
        compiler_params=_cparams(("arbitrary", "arbitrary"), VMEM_LIMIT),
        name="out_proj",
    )(mix_a, mix_b, mix_c, w_all, w_all, w_all, x_all, mod)


def kernel(x, c, ctx, c_ctx, w_ada, b_ada, norm_g, w_in, conv_w, conv_b, ln_g, ln_b, w_pw, b_pw,
           rconv_w, rconv_b, w_r, b_r, w_i, b_i, lam, rpb, w_out, final_g):
    bsz, s, d = x.shape
    n_ctx = ctx.shape[1]
    depth = w_ada.shape[0]
    w_conv = conv_w.shape[2]
    w_rec = rconv_w.shape[3]
    w_att = rpb.shape[1] * HEAD_DIM
    assert bsz == 1 and n_ctx == ROW_TILE and s % ROW_TILE == 0 and ATT_ROWS * GRID_W == ROW_TILE
    assert w_conv == w_rec == 1024
    rows = s // GRID_W
    first_col = 3 * w_conv + 2 * w_rec
    col0 = (0, w_att, 2 * w_att, 3 * w_att)
    conv_cb0 = 4 * w_att // w_conv
    rec_cb0 = conv_cb0 + 3

    cc = jnp.zeros((8, d), F32).at[0].set(c[0]).at[1].set(c_ctx)
    mods = _ada_all_layers(cc, w_ada, b_ada)

    for l in range(depth):
        mod = mods[l]
        if l == 0:
            h, x_all = _norm_mod_first(ctx[0], x[0], norm_g[l], mod.reshape(8, 1, 3 * d))
        else:
            h = _norm_mod(x_all, norm_g[l], mod.reshape(8, 1, 3 * d))
        z = _in_proj(h, w_in, l, first_col)
        mix_a = _conv_mixer(z, conv_cb0, conv_w[l], conv_b[l], ln_g[l], ln_b[l], w_pw[l].astype(BF16), b_pw[l])
        hf = _rglru(z, rec_cb0, None, rconv_w[l, 0], rconv_b[l, 0], w_r[l, 0], w_i[l, 0], b_r[l, 0],
                    b_i[l, 0], lam[l, 0], reverse=False)
        mix_b = _rglru(z, rec_cb0, hf, rconv_w[l, 1], rconv_b[l, 1], w_r[l, 1], w_i[l, 1], b_r[l, 1],
                       b_i[l, 1], lam[l, 1], reverse=True)
        mix_c = _attention(z, _attn_tables(rpb[l], rows), w_att, col0)
        x_all = _out_proj(mix_a, mix_b, mix_c, w_out, l, x_all, mod, n_ctx)

    return _final_norm(x_all, final_g, n_ctx // ROW_TILE)[None]
```

```python
import functools

import numpy as np
import jax
import jax.numpy as jnp
from jax import lax
from jax.experimental import pallas as pl
from jax.experimental.pallas import tpu as pltpu

F32 = jnp.float32
BF16 = jnp.bfloat16

GRID_W = 64
WIN_H = 8
WIN_W = 16
CONV_K = 31
REC_CONV_K = 4
REC_HEADS = 8
REC_C = 8.0
HEAD_DIM = 128
EPS = 1e-6
NEG = -1e30
LOG2E = 1.4426950408889634
LANES = 128

ROW_TILE = 256
HALO = 16
ATT_ROWS = 4
ATT_HEADS_PER_STEP = 16
VMEM_LIMIT = 56 * 1024 * 1024


def _cparams(sem, vmem=None):
    return pltpu.CompilerParams(dimension_semantics=sem, vmem_limit_bytes=vmem)


def _sigmoid(x):
    return 1.0 / (1.0 + jnp.exp2(x * -LOG2E))


def _silu(x):
    return x * _sigmoid(x)


def _ada_kernel(cc_ref, w_ref, b_ref, o_ref):
    a = _silu(cc_ref[...]).astype(BF16)
    w = w_ref[0].astype(BF16)
    o_ref[0] = jnp.dot(a, w, preferred_element_type=F32) + b_ref[0]


def _ada_all_layers(cc, w_ada, b_ada):
    depth, d, n = w_ada.shape
    tn = 512
    return pl.pallas_call(
        _ada_kernel,
        out_shape=jax.ShapeDtypeStruct((depth, 8, n), F32),
        grid=(depth, n // tn),
        in_specs=[pl.BlockSpec((8, d), lambda l, j: (0, 0)),
                  pl.BlockSpec((1, d, tn), lambda l, j: (l, 0, j)),
                  pl.BlockSpec((1, 1, tn), lambda l, j: (l, 0, j))],
        out_specs=pl.BlockSpec((1, 8, tn), lambda l, j: (l, 0, j)),
        compiler_params=_cparams(("arbitrary", "arbitrary"), VMEM_LIMIT),
        name="ada_mod",
    )(cc, w_ada, b_ada.reshape(depth, 1, n))


NORM_ROWS = 16


def _norm_mod_rows(src_ref, g_ref, mod_ref, h_ref, sc_scr, sh_scr, copy_ref, *, d):
    mod = mod_ref[0]
    sc_scr[...] = g_ref[...] * (1.0 + mod[:, d:2 * d])
    sh_scr[...] = mod[:, :d]

    def trip(r, carry):
        rows = pl.ds(pl.multiple_of(r * NORM_ROWS, NORM_ROWS), NORM_ROWS)
        x = src_ref[rows, :]
        if copy_ref is not None:
            copy_ref[rows, :] = x
        rs = lax.rsqrt(jnp.mean(x * x, axis=-1, keepdims=True) + EPS)
        h_ref[rows, :] = (src_ref[rows, :] * rs * sc_scr[...] + sh_scr[...]).astype(h_ref.dtype)
        return carry

    lax.fori_loop(0, src_ref.shape[0] // NORM_ROWS, trip, 0, unroll=4)


def _norm_mod_kernel(x_ref, g_ref, mod_ref, o_ref, sc_scr, sh_scr, *, d, ctx_trips):
    i = pl.program_id(0)
    for which in range(2):
        mod = mod_ref[which]
        sc_scr[which:which + 1, :] = g_ref[...] * (1.0 + mod[:, d:2 * d])
        sh_scr[which:which + 1, :] = mod[:, :d]

    def trip(r, carry):
        rows = pl.ds(pl.multiple_of(r * NORM_ROWS, NORM_ROWS), NORM_ROWS)
        which = pl.ds(jnp.where((i == 0) & (r < ctx_trips), 1, 0), 1)
        x = x_ref[rows, :]
        rs = lax.rsqrt(jnp.mean(x * x, axis=-1, keepdims=True) + EPS)
        o_ref[rows, :] = (x_ref[rows, :] * rs * sc_scr[which, :] + sh_scr[which, :]).astype(o_ref.dtype)
        return carry

    lax.fori_loop(0, x_ref.shape[0] // NORM_ROWS, trip, 0, unroll=4)


def _norm_mod(x_all, g, mod3, n_ctx):
    sa, d = x_all.shape
    tile = max(t for t in range(4 * NORM_ROWS, 640 + 1, 4 * NORM_ROWS) if sa % t == 0)
    assert tile >= n_ctx and n_ctx % NORM_ROWS == 0
    return pl.pallas_call(
        functools.partial(_norm_mod_kernel, d=d, ctx_trips=n_ctx // NORM_ROWS),
        out_shape=jax.ShapeDtypeStruct((sa, d), BF16),
        grid=(sa // tile,),
        in_specs=[pl.BlockSpec((tile, d), lambda i: (i, 0)),
                  pl.BlockSpec((1, d), lambda i: (0, 0)),
                  pl.BlockSpec((2, 1, 3 * d), lambda i: (0, 0, 0))],
        out_specs=pl.BlockSpec((tile, d), lambda i: (i, 0)),
        scratch_shapes=[pltpu.VMEM((2, d), F32), pltpu.VMEM((2, d), F32)],
        compiler_params=_cparams(("arbitrary",), VMEM_LIMIT),
        name="norm_mod",
    )(x_all, g.reshape(1, d), mod3)


def _norm_mod_first_kernel(ctx_ref, x_ref, g_ref, mod_ref, h_ref, xall_ref, sc_scr, sh_scr, *, d):
    emit = lambda src_ref: _norm_mod_rows(src_ref, g_ref, mod_ref, h_ref, sc_scr, sh_scr, xall_ref, d=d)
    pl.when(pl.program_id(0) == 0)(lambda: emit(ctx_ref))
    pl.when(pl.program_id(0) > 0)(lambda: emit(x_ref))


def _norm_mod_first(ctx2, x2, g, mod3):
    n_ctx, d = ctx2.shape
    sa = n_ctx + x2.shape[0]
    assert n_ctx == ROW_TILE
    return pl.pallas_call(
        functools.partial(_norm_mod_first_kernel, d=d),
        out_shape=(jax.ShapeDtypeStruct((sa, d), BF16), jax.ShapeDtypeStruct((sa, d), F32)),
        grid=(sa // ROW_TILE,),
        in_specs=[pl.BlockSpec((ROW_TILE, d), lambda i: (0, 0)),
                  pl.BlockSpec((ROW_TILE, d), lambda i: (jnp.maximum(i - 1, 0), 0)),
                  pl.BlockSpec((1, d), lambda i: (0, 0)),
                  pl.BlockSpec((1, 1, 3 * d), lambda i: (jnp.where(i == 0, 1, 0), 0, 0))],
        out_specs=(pl.BlockSpec((ROW_TILE, d), lambda i: (i, 0)),
                   pl.BlockSpec((ROW_TILE, d), lambda i: (i, 0))),
        scratch_shapes=[pltpu.VMEM((1, d), F32), pltpu.VMEM((1, d), F32)],
        compiler_params=_cparams(("arbitrary",), VMEM_LIMIT),
        name="norm_mod_first",
    )(ctx2, x2, g.reshape(1, d), mod3)


def _final_norm_kernel(x_ref, g_ref, o_ref):
    x = x_ref[...]
    o_ref[...] = x * lax.rsqrt(jnp.mean(x * x, axis=-1, keepdims=True) + EPS) * g_ref[...]


def _final_norm(x_all, g, n_ctx_tiles):
    sa, d = x_all.shape
    s = sa - n_ctx_tiles * ROW_TILE
    return pl.pallas_call(
        _final_norm_kernel,
        out_shape=jax.ShapeDtypeStruct((s, d), F32),
        grid=(s // ROW_TILE,),
        in_specs=[pl.BlockSpec((ROW_TILE, d), lambda i: (i + n_ctx_tiles, 0)),
                  pl.BlockSpec((1, d), lambda i: (0, 0))],
        out_specs=pl.BlockSpec((ROW_TILE, d), lambda i: (i, 0)),
        compiler_params=_cparams(("arbitrary",), VMEM_LIMIT),
        name="final_norm",
    )(x_all, g.reshape(1, d))


def _matmul_kernel(a_ref, b_ref, o_ref):
    o_ref[...] = jnp.dot(a_ref[...], b_ref[0].astype(BF16), preferred_element_type=F32).astype(o_ref.dtype)


def _row_tile(sa, cap):
    best = ROW_TILE
    for t in range(ROW_TILE, cap + 1, ROW_TILE):
        if sa % t == 0:
            best = t
    return best


def _in_proj(h, w_all, l, first_col):
    sa, d = h.shape
    n = w_all.shape[2]
    tm, tn = _row_tile(sa, 1280), 512
    assert first_col % tn == 0
    rot, n_col_blocks = first_col // tn, n // tn
    return pl.pallas_call(
        _matmul_kernel,
        out_shape=jax.ShapeDtypeStruct((sa, n), BF16),
        grid=(sa // tm, n // tn),
        in_specs=[pl.BlockSpec((tm, d), lambda i, j: (i, 0)),
                  pl.BlockSpec((1, d, tn), lambda i, j: (l, 0, (j + rot) % n_col_blocks))],
        out_specs=pl.BlockSpec((tm, tn), lambda i, j: (i, j)),
        compiler_params=_cparams(("arbitrary", "arbitrary"), VMEM_LIMIT),
        name="in_proj",
    )(h, w_all)


SUB = 8


def _shift_rows(x3, d, edge, up):
    rows = lax.broadcasted_iota(jnp.int32, x3.shape, 1)
    if up:
        after = jnp.concatenate([x3[1:], edge[None]], axis=0)
        return pltpu.roll(jnp.where(rows >= d, x3, after), SUB - d, 1)
    before = jnp.concatenate([edge[None], x3[:-1]], axis=0)
    return pltpu.roll(jnp.where(rows < SUB - d, x3, before), d, 1)


def _conv_kernel(val_ref, glu_ref, gate_ref, vb_ref, gb_ref, va_ref, ga_ref,
                 cw_ref, cb_ref, lng_ref, lnb_ref, wpw_ref, bpw_ref, o_ref, u_scr, *, t):
    i = pl.program_id(0)
    n = pl.num_programs(0)

    def glu(v_ref, g_ref):
        return v_ref[...].astype(F32) * _sigmoid(g_ref[...].astype(F32))

    has_before = jnp.where(i >= 2, 1.0, 0.0)
    has_after = jnp.where((i >= 1) & (i < n - 1), 1.0, 0.0)
    u_scr[0:HALO, :] = glu(vb_ref, gb_ref) * has_before
    u_scr[HALO:HALO + t, :] = glu(val_ref, glu_ref)
    u_scr[HALO + t:HALO + t + HALO, :] = glu(va_ref, ga_ref) * has_after

    first = HALO - CONV_K // 2
    groups = t // SUB
    blocks = []
    for lb in range(u_scr.shape[1] // LANES):
        lanes = slice(lb * LANES, (lb + 1) * LANES)
        acc = None
        for res in range(SUB):
            g = None
            for off in range(res, first + CONV_K, SUB):
                if off < first:
                    continue
                term = cw_ref[off - first:off - first + 1, lanes] * u_scr[pl.ds(off - res, t + SUB), lanes]
                g = term if g is None else g + term
            g = g.reshape(groups + 1, SUB, LANES)
            g = _shift_rows(g[:groups], res, g[groups], up=True) if res else g[:groups]
            acc = g if acc is None else acc + g
        blocks.append(acc.reshape(t, LANES) + cb_ref[:, lanes])
    acc = jnp.concatenate(blocks, axis=1)

    mu = jnp.mean(acc, axis=-1, keepdims=True)
    cen = acc - mu
    var = jnp.mean(cen * cen, axis=-1, keepdims=True)
    y = _silu(cen * lax.rsqrt(var + EPS) * lng_ref[...] + lnb_ref[...])
    out = jnp.dot(y.astype(BF16), wpw_ref[...], preferred_element_type=F32) + bpw_ref[...]
    o_ref[...] = (out * _silu(gate_ref[...].astype(F32))).astype(o_ref.dtype)


def _conv_mixer(z, cb0, cw, cb, lng, lnb, wpw, bpw):
    sa = z.shape[0]
    c = cw.shape[1]
    t = ROW_TILE
    hb = t // HALO
    last_h = sa // HALO - 1
    cw_p = jnp.zeros((32, c), F32).at[:CONV_K].set(cw)
    row = lambda v: v.reshape(1, c)
    before = lambda col: pl.BlockSpec((HALO, c), lambda i: (jnp.maximum(i * hb - 1, 0), col))
    after = lambda col: pl.BlockSpec((HALO, c), lambda i: (jnp.minimum((i + 1) * hb, last_h), col))
    full = lambda shape: pl.BlockSpec(shape, lambda i: (0, 0))
    return pl.pallas_call(
        functools.partial(_conv_kernel, t=t),
        out_shape=jax.ShapeDtypeStruct((sa, c), BF16),
        grid=(sa // t,),
        in_specs=[pl.BlockSpec((t, c), lambda i: (i, cb0)),
                  pl.BlockSpec((t, c), lambda i: (i, cb0 + 1)),
                  pl.BlockSpec((t, c), lambda i: (i, cb0 + 2)),
                  before(cb0), before(cb0 + 1), after(cb0), after(cb0 + 1),
                  full((32, c)), full((1, c)), full((1, c)), full((1, c)),
                  full((c, c)), full((1, c))],
        out_specs=pl.BlockSpec((t, c), lambda i: (i, 0)),
        scratch_shapes=[pltpu.VMEM((t + 2 * HALO, c), F32)],
        compiler_params=_cparams(("arbitrary",), VMEM_LIMIT),
        name="conv_mixer",
    )(z, z, z, z, z, z, z, cw_p, row(cb), row(lng), row(lnb), wpw, row(bpw))


def _scan_tile(a, b, h_in, reverse):
    t, c = a.shape
    groups = t // SUB
    a = a.reshape(groups, SUB, c)
    b = b.reshape(groups, SUB, c)
    rows = lax.broadcasted_iota(jnp.int32, a.shape, 1)
    s = 1
    while s < SUB:
        if reverse:
            keep = rows < SUB - s
            shift = SUB - s
        else:
            keep = rows >= s
            shift = s
        a_sh = jnp.where(keep, pltpu.roll(a, shift, 1), 1.0)
        b_sh = jnp.where(keep, pltpu.roll(b, shift, 1), 0.0)
        b = a * b_sh + b
        a = a * a_sh
        s *= 2
    out = [None] * groups
    h = h_in
    for g in (range(groups - 1, -1, -1) if reverse else range(groups)):
        hg = a[g] * h + b[g]
        out[g] = hg
        h = hg[0:1, :] if reverse else hg[SUB - 1:SUB, :]
    return jnp.concatenate(out, axis=0), h


def _rglru_kernel(*refs, t, reverse):
    if reverse:
        (x_ref, xh_ref, hf_ref, gate_ref, cw_ref, cb_ref, wri_ref, br_ref, bi_ref, lam_ref,
         o_ref, carry_scr) = refs
    else:
        (x_ref, xh_ref, cw_ref, cb_ref, wri_ref, br_ref, bi_ref, lam_ref,
         o_ref, carry_scr) = refs
    s = pl.program_id(0)

    @pl.when(s == 0)
    def _():
        carry_scr[...] = jnp.zeros_like(carry_scr)

    c = x_ref.shape[1]
    has_halo = jnp.where(s >= 2, 1.0, 0.0)
    x3 = x_ref[...].astype(F32).reshape(t // SUB, SUB, c)
    halo = xh_ref[...].astype(F32) * has_halo
    edge = halo[0:SUB] if reverse else halo[HALO - SUB:HALO]
    xc = None
    for k in range(REC_CONV_K):
        d = k if reverse else REC_CONV_K - 1 - k
        term = cw_ref[k:k + 1, :] * (_shift_rows(x3, d, edge, up=reverse) if d else x3)
        xc = term if xc is None else xc + term
    xc = xc.reshape(t, c) + cb_ref[...]

    xcb = xc.astype(BF16)
    blk = c // REC_HEADS
    r_parts, i_parts = [], []
    for hh in range(REC_HEADS):
        ri = jnp.dot(xcb[:, hh * blk:(hh + 1) * blk], wri_ref[hh], preferred_element_type=F32)
        r_parts.append(ri[:, :blk])
        i_parts.append(ri[:, blk:])
    r = _sigmoid(jnp.concatenate(r_parts, axis=1) + br_ref[...])
    ig = _sigmoid(jnp.concatenate(i_parts, axis=1) + bi_ref[...])

    lam = lam_ref[...]
    softplus_neg_lam = jnp.maximum(-lam, 0.0) + jnp.log(1.0 + jnp.exp(-jnp.abs(lam)))
    a = jnp.exp2(r * (-REC_C * LOG2E * softplus_neg_lam))
    one_m_a2 = 1.0 - a * a
    b = (one_m_a2 * lax.rsqrt(jnp.maximum(one_m_a2, 1e-30))) * (ig * xc)

    h, h_out = _scan_tile(a, b, carry_scr[0:1, :], reverse)
    carry_scr[0:1, :] = h_out

    if reverse:
        y = hf_ref[...].astype(F32) + h
        o_ref[...] = (y * _silu(gate_ref[...].astype(F32))).astype(o_ref.dtype)
    else:
        o_ref[...] = h.astype(o_ref.dtype)


def _rglru(z, cb0, hf, cw, cb, w_r, w_i, b_r, b_i, lam, reverse):
    sa = z.shape[0]
    c = cw.shape[1]
    t = ROW_TILE
    n = sa // t
    hb = t // HALO
    last_h = sa // HALO - 1
    wri = jnp.concatenate([w_r, w_i], axis=-1).astype(BF16)
    cw_p = jnp.zeros((8, c), F32).at[:REC_CONV_K].set(cw)
    row = lambda v: v.reshape(1, c)
    full = lambda shape: pl.BlockSpec(shape, lambda s: (0,) * len(shape))
    if reverse:
        tile = lambda s: jnp.where(s == 0, 0, n - s)
        halo_spec = pl.BlockSpec((HALO, c), lambda s: (jnp.minimum((tile(s) + 1) * hb, last_h), cb0))
    else:
        tile = lambda s: s
        halo_spec = pl.BlockSpec((HALO, c), lambda s: (jnp.maximum(tile(s) * hb - 1, 0), cb0))
    in_specs = [pl.BlockSpec((t, c), lambda s: (tile(s), cb0)), halo_spec]
    args = [z, z]
    if reverse:
        in_specs += [pl.BlockSpec((t, c), lambda s: (tile(s), 0)),
                     pl.BlockSpec((t, c), lambda s: (tile(s), cb0 + 1))]
        args += [hf, z]
    in_specs += [full((8, c)), full((1, c)), full(wri.shape), full((1, c)), full((1, c)), full((1, c))]
    args += [cw_p, row(cb), wri, row(b_r), row(b_i), row(lam)]
    return pl.pallas_call(
        functools.partial(_rglru_kernel, t=t, reverse=reverse),
        out_shape=jax.ShapeDtypeStruct((sa, c), BF16),
        grid=(n,),
        in_specs=in_specs,
        out_specs=pl.BlockSpec((t, c), lambda s: (tile(s), 0)),
        scratch_shapes=[pltpu.VMEM((8, c), F32)],
        compiler_params=_cparams(("arbitrary",), VMEM_LIMIT),
        name="rglru_bwd" if reverse else "rglru_fwd",
    )(*args)


def _row_validity(rows):
    r4 = ATT_ROWS
    n_blocks = rows // r4
    assert rows % r4 == 0 and rows >= 3 * r4 and rows >= WIN_H

    def for_block(b):
        rq = np.arange(r4)[:, None]
        jr = np.arange(3 * r4)[None, :]
        r = r4 * b + rq
        rk = r4 * b - r4 + jr
        rs = np.clip(r - WIN_H // 2, 0, rows - WIN_H)
        return (rk >= rs) & (rk < rs + WIN_H) & (rk >= 0) & (rk < rows)

    return np.stack([for_block(0), for_block(1), for_block(n_blocks - 1), np.zeros((r4, 3 * r4), bool)])


def _table_kernel(rp_ref, cm_ref, o_ref, *, row_ok):
    w = GRID_W
    n_dr = 2 * WIN_H - 1
    lane = lax.broadcasted_iota(jnp.int32, (w, 2 * w), 1)
    left = lane < w
    cm = cm_ref[...]
    lo, hi = [], []
    for dr in range(n_dr):
        row = jnp.broadcast_to(rp_ref[0, dr:dr + 1, :], (w, 2 * w))
        lo.append(pltpu.roll(row, 0, 1, stride=1, stride_axis=0) + cm)
        hi.append(pltpu.roll(row, w, 1, stride=1, stride_axis=0) + cm)
    neg = jnp.full((w, 2 * w), NEG, F32)
    n_var, r4, n_jr = row_ok.shape
    for v in range(n_var):
        for rq in range(r4):
            for m in range(n_jr // 2):
                dr = 2 * m - rq + (WIN_H - 1) - r4
                ok_a, ok_b = bool(row_ok[v, rq, 2 * m]), bool(row_ok[v, rq, 2 * m + 1])
                if ok_a and ok_b:
                    tile = jnp.where(left, lo[dr], hi[dr + 1])
                elif ok_a:
                    tile = jnp.where(left, lo[dr], neg)
                elif ok_b:
                    tile = jnp.where(left, neg, hi[dr + 1])
                else:
                    tile = neg
                o_ref[v, 0, rq * w:(rq + 1) * w, m * 2 * w:(m + 1) * 2 * w] = tile


def _attn_tables(rpb, rows):
    n_heads = rpb.shape[0]
    w = GRID_W
    row_ok = _row_validity(rows)
    cq = np.arange(w)[:, None]
    ck = np.arange(w)[None, :]
    cs = np.clip(cq - WIN_W // 2, 0, w - WIN_W)
    col_mask = np.where((ck >= cs) & (ck < cs + WIN_W), 0.0, NEG).astype(np.float32)
    col_mask = np.concatenate([col_mask, col_mask], axis=1)
    rp = jnp.concatenate([rpb[..., WIN_W - 1:], jnp.zeros(rpb.shape[:2] + (2 * w - (2 * WIN_W - 1),), F32),
                          rpb[..., :WIN_W - 1]], axis=-1) * LOG2E
    rp = jnp.pad(rp, ((0, 0), (0, 1), (0, 0)))
    return pl.pallas_call(
        functools.partial(_table_kernel, row_ok=row_ok),
        out_shape=jax.ShapeDtypeStruct((row_ok.shape[0], n_heads, ROW_TILE, 3 * ROW_TILE), F32),
        grid=(n_heads,),
        in_specs=[pl.BlockSpec((1, 16, 2 * w), lambda h: (h, 0, 0)),
                  pl.BlockSpec((w, 2 * w), lambda h: (0, 0))],
        out_specs=pl.BlockSpec((row_ok.shape[0], 1, ROW_TILE, 3 * ROW_TILE), lambda h: (0, h, 0, 0)),
        compiler_params=_cparams(("arbitrary",), VMEM_LIMIT),
        name="attn_tables",
    )(rp, jnp.asarray(col_mask))


def _attn_kernel(q_ref, k0_ref, k1_ref, k2_ref, v0_ref, v1_ref, v2_ref, kc_ref, vc_ref, g_ref,
                 t_ref, o_ref, *, heads, scale):
    nt = (((1,), (1,)), ((), ()))
    t = q_ref.shape[0]
    ones_loc = jnp.ones((3 * t, HEAD_DIM), BF16)
    ones_ctx = jnp.ones((kc_ref.shape[0], HEAD_DIM), BF16)

    def scores(h):
        sl = slice(h * HEAD_DIM, (h + 1) * HEAD_DIM)
        q = (q_ref[:, sl].astype(F32) * (scale * LOG2E)).astype(BF16)
        k = jnp.concatenate([k0_ref[:, sl], k1_ref[:, sl], k2_ref[:, sl]], axis=0)
        s_loc = lax.dot_general(q, k, nt, preferred_element_type=F32) + t_ref[0, h]
        s_ctx = lax.dot_general(q, kc_ref[:, sl], nt, preferred_element_type=F32)
        return s_loc, s_ctx

    nxt = scores(0)
    for h in range(heads):
        sl = slice(h * HEAD_DIM, (h + 1) * HEAD_DIM)
        s_loc, s_ctx = nxt
        if h + 1 < heads:
            nxt = scores(h + 1)
        v = jnp.concatenate([v0_ref[:, sl], v1_ref[:, sl], v2_ref[:, sl]], axis=0)
        m = jnp.maximum(jnp.max(s_loc, axis=-1, keepdims=True), jnp.max(s_ctx, axis=-1, keepdims=True))
        p_loc = jnp.exp2(s_loc - m).astype(BF16)
        p_ctx = jnp.exp2(s_ctx - m).astype(BF16)
        od = (jnp.dot(p_loc, jnp.concatenate([v, ones_loc], axis=1), preferred_element_type=F32)
              + jnp.dot(p_ctx, jnp.concatenate([vc_ref[:, sl], ones_ctx], axis=1), preferred_element_type=F32))
        o = od[:, :HEAD_DIM] / od[:, HEAD_DIM:]
        o_ref[:, sl] = (o * _silu(g_ref[:, sl].astype(F32))).astype(o_ref.dtype)


def _attention(z, tables, w_att, col0):
    sa = z.shape[0]
    t = ROW_TILE
    nb = sa // t - 1
    hg = ATT_HEADS_PER_STEP
    wl = hg * HEAD_DIM
    n_heads = w_att // HEAD_DIM
    assert all(c % wl == 0 for c in col0)
    cq, ckk, cv, cg = (c // wl for c in col0)
    clampb = lambda s: jnp.clip(s, 1, nb)
    variant = lambda s: jnp.where(s == 0, 3, jnp.where(s == 1, 0, jnp.where(s == nb, 2, 1)))
    blk = lambda rowf, colb: pl.BlockSpec((t, wl), lambda g, s: (rowf(s), colb + g))
    return pl.pallas_call(
        functools.partial(_attn_kernel, heads=hg, scale=HEAD_DIM ** -0.5),
        out_shape=jax.ShapeDtypeStruct((sa, w_att), BF16),
        grid=(n_heads // hg, nb + 1),
        in_specs=[blk(lambda s: s, cq),
                  blk(lambda s: clampb(s - 1), ckk), blk(lambda s: clampb(s), ckk), blk(lambda s: clampb(s + 1), ckk),
                  blk(lambda s: clampb(s - 1), cv), blk(lambda s: clampb(s), cv), blk(lambda s: clampb(s + 1), cv),
                  blk(lambda s: 0, ckk), blk(lambda s: 0, cv),
                  blk(lambda s: s, cg),
                  pl.BlockSpec((1, hg, t, 3 * t), lambda g, s: (variant(s), g, 0, 0))],
        out_specs=pl.BlockSpec((t, wl), lambda g, s: (s, g)),
        compiler_params=_cparams(("arbitrary", "arbitrary"), VMEM_LIMIT),
        name="nbr_attention",
    )(z, z, z, z, z, z, z, z, z, z, tables)


def _out_proj_kernel(a_ref, b_ref, c_ref, wa_ref, wb_ref, wc_ref, x_ref, g_ref, o_ref, *, tm, n_ctx):
    i = pl.program_id(0)
    acc = (jnp.dot(a_ref[...], wa_ref[0].astype(BF16), preferred_element_type=F32)
           + jnp.dot(b_ref[...], wb_ref[0].astype(BF16), preferred_element_type=F32)
           + jnp.dot(c_ref[...], wc_ref[0].astype(BF16), preferred_element_type=F32))
    rows = i * tm + lax.broadcasted_iota(jnp.int32, acc.shape, 0)
    g = jnp.where(rows < n_ctx, g_ref[1:2, :], g_ref[0:1, :])
    o_ref[...] = x_ref[...] + g * acc


def _out_proj(mix_a, mix_b, mix_c, w_all, l, x_all, mod, n_ctx):
    sa, d = x_all.shape
    wa, wb, wc = mix_a.shape[1], mix_b.shape[1], mix_c.shape[1]
    assert wa == wb and wc % wa == 0
    tm, tn = _row_tile(sa, 1280), 512
    gate_blk0 = 2 * d // tn
    return pl.pallas_call(
        functools.partial(_out_proj_kernel, tm=tm, n_ctx=n_ctx),
        out_shape=jax.ShapeDtypeStruct((sa, d), F32),
        grid=(sa // tm, d // tn),
        in_specs=[pl.BlockSpec((tm, wa), lambda i, j: (i, 0)),
                  pl.BlockSpec((tm, wb), lambda i, j: (i, 0)),
                  pl.BlockSpec((tm, wc), lambda i, j: (i, 0)),
                  pl.BlockSpec((1, wa, tn), lambda i, j: (l, 0, j)),
                  pl.BlockSpec((1, wb, tn), lambda i, j: (l, 1, j)),
                  pl.BlockSpec((1, wc, tn), lambda i, j: (l, (wa + wb) // wc, j)),
                  pl.BlockSpec((tm, tn), lambda i, j: (i, j)),
                  pl.BlockSpec((8, tn), lambda i, j: (0, gate_blk0 + j))],
        out_specs=pl.BlockSpec((tm, tn), lambda i, j: (i, j)),
        compiler_params=_cparams(("arbitrary", "arbitrary"), VMEM_LIMIT),
        name="out_proj",
    )(mix_a, mix_b, mix_c, w_all, w_all, w_all, x_all, mod)


def kernel(x, c, ctx, c_ctx, w_ada, b_ada, norm_g, w_in, conv_w, conv_b, ln_g, ln_b, w_pw, b_pw,
           rconv_w, rconv_b, w_r, b_r, w_i, b_i, lam, rpb, w_out, final_g):
    bsz, s, d = x.shape
    n_ctx = ctx.shape[1]
    depth = w_ada.shape[0]
    w_conv = conv_w.shape[2]
    w_rec = rconv_w.shape[3]
    w_att = rpb.shape[1] * HEAD_DIM
    assert bsz == 1 and n_ctx == ROW_TILE and s % ROW_TILE == 0 and ATT_ROWS * GRID_W == ROW_TILE
    assert w_conv == w_rec == 1024
    rows = s // GRID_W
    first_col = 3 * w_conv + 2 * w_rec
    col0 = (0, w_att, 2 * w_att, 3 * w_att)
    conv_cb0 = 4 * w_att // w_conv
    rec_cb0 = conv_cb0 + 3

    cc = jnp.zeros((8, d), F32).at[0].set(c[0]).at[1].set(c_ctx)
    mods = _ada_all_layers(cc, w_ada, b_ada)

    for l in range(depth):
        mod = mods[l]
        if l == 0:
            h, x_all = _norm_mod_first(ctx[0], x[0], norm_g[l], mod.reshape(8, 1, 3 * d))
        else:
            h = _norm_mod(x_all, norm_g[l], mod.reshape(8, 1, 3 * d), n_ctx)
        z = _in_proj(h, w_in, l, first_col)
        mix_a = _conv_mixer(z, conv_cb0, conv_w[l], conv_b[l], ln_g[l], ln_b[l], w_pw[l].astype(BF16), b_pw[l])
        hf = _rglru(z, rec_cb0, None, rconv_w[l, 0], rconv_b[l, 0], w_r[l, 0], w_i[l, 0], b_r[l, 0],
                    b_i[l, 0], lam[l, 0], reverse=False)
        mix_b = _rglru(z, rec_cb0, hf, rconv_w[l, 1], rconv_b[l, 1], w_r[l, 1], w_i[l, 1], b_r[l, 1],
                       b_i[l, 1], lam[l, 1], reverse=True)
        mix_c = _attention(z, _attn_tables(rpb[l], rows), w_att, col0)
        x_all = _out_proj(mix_a, mix_b, mix_c, w_out, l, x_all, mod, n_ctx)

    return _final_norm(x_all, final_g, n_ctx // ROW_TILE)[None]
```
